```python
import jax
import jax.numpy as jnp
from jax import lax
import numpy as np

D_MODEL = 4096
BATCH = 1
SEQ = 16384
DEPTH = 4
DEC_BATCH = 16
DEC_SEQ = 16
PAST_LEN = 2048

CHUNK = 64
QBLOCK = 128
N_BRANCH = 4
BRANCH_W = 1024
EPS = 1e-6

W_A = BRANCH_W
LRU_BLOCKS = 8
LRU_BLOCK = W_A // LRU_BLOCKS
CONV_A = 4
LRU_C = 8.0
W_B = BRANCH_W
CONV_B = 3
W_C = BRANCH_W
HEAD_DIM_C = 64
N_HEADS_C = W_C // HEAD_DIM_C
N_GROUPS_C = 2
HEADS_PER_GROUP_C = N_HEADS_C // N_GROUPS_C
D_STATE = 128
CONV_C = 4
CONV_DIM_C = W_C + 2 * N_GROUPS_C * D_STATE
N_HEADS_D = 8
NOPE_DIM = 128
ROPE_DIM = 64
V_DIM = 128
Q_LORA = 1024
KV_LORA = 512
W_D = N_HEADS_D * V_DIM
ROPE_THETA = 10000.0
ATTN_SCALE = (NOPE_DIM + ROPE_DIM) ** -0.5

IN_WIDTHS = (W_A, W_A,
             W_B, W_B, W_B, W_B,
             W_C, N_GROUPS_C * D_STATE, N_GROUPS_C * D_STATE, N_HEADS_C, W_C,
             Q_LORA, KV_LORA, ROPE_DIM, W_D,
             N_BRANCH * D_MODEL)
D_IN = 2 * W_A + 4 * W_B + 2 * W_C + 2 * N_GROUPS_C * D_STATE + N_HEADS_C + Q_LORA + KV_LORA + ROPE_DIM + W_D + N_BRANCH * D_MODEL

kernel_name = 'hybrid_stream_lru_conv_ssd_mla'


def _split_points():
    return [int(v) for v in np.cumsum(np.array(IN_WIDTHS))[:-1]]


def rmsnorm(x, w):
    xf = x.astype(jnp.float32)
    y = xf * lax.rsqrt(jnp.mean(xf * xf, axis=-1, keepdims=True) + EPS)
    return (y * w.astype(jnp.float32)).astype(x.dtype)


def gated_group_rmsnorm(y, z, w):
    g = y.astype(jnp.float32) * jax.nn.silu(z.astype(jnp.float32))
    gr = g.reshape(g.shape[:-1] + (N_GROUPS_C, W_C // N_GROUPS_C))
    gr = gr * lax.rsqrt(jnp.mean(gr * gr, axis=-1, keepdims=True) + EPS)
    return (gr.reshape(g.shape) * w.astype(jnp.float32)).astype(z.dtype)


def causal_conv(x, prev, w, b=None):
    width, seqlen = w.shape[0], x.shape[1]
    xp = jnp.concatenate([prev.astype(x.dtype), x], axis=1)
    y = xp[:, 0:seqlen] * w[0]
    for j in range(1, width):
        y = y + xp[:, j:j + seqlen] * w[j]
    if b is not None:
        y = y + b
    return y, xp[:, seqlen:]


def rope(x, pos):
    half = ROPE_DIM // 2
    inv = ROPE_THETA ** (-jnp.arange(half, dtype=jnp.float32) / half)
    ang = pos.astype(jnp.float32)[:, None] * inv[None, :]
    cos, sin = jnp.cos(ang), jnp.sin(ang)
    if x.ndim == 4:
        cos, sin = cos[:, None, :], sin[:, None, :]
    xf = x.astype(jnp.float32)
    x1, x2 = xf[..., :half], xf[..., half:]
    return jnp.concatenate([x1 * cos - x2 * sin, x2 * cos + x1 * sin], axis=-1).astype(x.dtype)


def rglru(xc, h0, w_r, b_r, w_i, b_i, lam):
    bsz, seqlen, _ = xc.shape
    xf = xc.astype(jnp.float32)
    xb = xf.reshape(bsz, seqlen, LRU_BLOCKS, LRU_BLOCK)
    r = jax.nn.sigmoid(jnp.einsum('blnc,ncd->blnd', xb, w_r.astype(jnp.float32)).reshape(bsz, seqlen, W_A) + b_r.astype(jnp.float32))
    i = jax.nn.sigmoid(jnp.einsum('blnc,ncd->blnd', xb, w_i.astype(jnp.float32)).reshape(bsz, seqlen, W_A) + b_i.astype(jnp.float32))
    log_a = -LRU_C * r * jax.nn.softplus(-lam.astype(jnp.float32))
    a = jnp.exp(log_a)
    u = jnp.sqrt(-jnp.expm1(2.0 * log_a)) * (i * xf)
    u = u.at[:, 0].add(a[:, 0] * h0.astype(jnp.float32))

    def combine(left, right):
        a1, b1 = left
        a2, b2 = right
        return a1 * a2, a2 * b1 + b2

    _, h = lax.associative_scan(combine, (a, u), axis=1)
    return h.astype(xc.dtype), h[:, -1].astype(h0.dtype)


def ssd(x, dt, a, bm, cm, s0):
    bsz, seqlen = x.shape[0], x.shape[1]
    cl = CHUNK if seqlen % CHUNK == 0 else seqlen
    nc = seqlen // cl
    f32 = jnp.float32
    xr = x.astype(f32).reshape(bsz, nc, cl, N_GROUPS_C, HEADS_PER_GROUP_C, HEAD_DIM_C)
    dtr = dt.astype(f32).reshape(bsz, nc, cl, N_GROUPS_C, HEADS_PER_GROUP_C)
    br = bm.astype(f32).reshape(bsz, nc, cl, N_GROUPS_C, D_STATE)
    cr = cm.astype(f32).reshape(bsz, nc, cl, N_GROUPS_C, D_STATE)
    acum = jnp.cumsum(dtr * a.astype(f32).reshape(N_GROUPS_C, HEADS_PER_GROUP_C), axis=2)
    xdt = xr * dtr[..., None]
    seg = acum[:, :, :, None] - acum[:, :, None, :]
    tri = jnp.tril(jnp.ones((cl, cl), dtype=bool))[:, :, None, None]
    decay_ls = jnp.exp(jnp.where(tri, seg, -jnp.inf))
    cb = jnp.einsum('bclgn,bcsgn->bclsg', cr, br)
    y_diag = jnp.einsum('bclsge,bcsgep->bclgep', cb[..., None] * decay_ls, xdt)
    decay_end = jnp.exp(acum[:, :, -1:] - acum)
    states = jnp.einsum('bclgn,bclgep->bcgepn', br, xdt * decay_end[..., None])
    chunk_decay = jnp.exp(acum[:, :, -1])

    def step(s, inp):
        st, dc = inp
        return s * dc[..., None, None] + st, s

    s_init = s0.astype(f32).reshape(bsz, N_GROUPS_C, HEADS_PER_GROUP_C, HEAD_DIM_C, D_STATE)
    s_fin, prev = lax.scan(step, s_init, (jnp.moveaxis(states, 1, 0), jnp.moveaxis(chunk_decay, 1, 0)))
    prev = jnp.moveaxis(prev, 0, 1)
    y_off = jnp.einsum('bclgn,bcgepn->bclgep', cr, prev) * jnp.exp(acum)[..., None]
    y = (y_diag + y_off).reshape(bsz, seqlen, N_HEADS_C, HEAD_DIM_C)
    return y, s_fin.reshape(bsz, N_HEADS_C, HEAD_DIM_C, D_STATE).astype(s0.dtype)


def attend(qn, qp, kn, kp, v, qpos, kpos):
    s = (jnp.einsum('bqhd,bkhd->bhqk', qn, kn, preferred_element_type=jnp.float32)
         + jnp.einsum('bqhr,bkr->bhqk', qp, kp, preferred_element_type=jnp.float32)) * ATTN_SCALE
    visible = kpos[None, :] < (qpos[:, None] // CHUNK + 1) * CHUNK
    p = jax.nn.softmax(jnp.where(visible, s, -jnp.inf), axis=-1)
    return jnp.einsum('bhqk,bkhd->bqhd', p.astype(v.dtype), v)


def attend_blocks(qn, qp, kn, kp, v, kpos):
    bsz, seqlen = qn.shape[0], qn.shape[1]
    nb = seqlen // QBLOCK
    qn_b = qn.reshape(bsz, nb, QBLOCK, N_HEADS_D, NOPE_DIM).swapaxes(0, 1)
    qp_b = qp.reshape(bsz, nb, QBLOCK, N_HEADS_D, ROPE_DIM).swapaxes(0, 1)

    def one(args):
        qn_j, qp_j, j = args
        qpos = j * QBLOCK + jnp.arange(QBLOCK)
        return attend(qn_j, qp_j, kn, kp, v, qpos, kpos)

    out = lax.map(one, (qn_b, qp_b, jnp.arange(nb)))
    return out.swapaxes(0, 1).reshape(bsz, seqlen, N_HEADS_D, V_DIM)


def layer(x, pos, conv_a_prev, h0, conv_b_prev, conv_c_prev, ssd0, past_ckv, past_kpe,
          pre_norm_w, w_in, conv_a_w, conv_a_b, lru_w_r, lru_b_r, lru_w_i, lru_b_i, lru_lambda,
          conv_b_w, conv_c_w, conv_c_b, ssd_dt_bias, ssd_a_log, ssd_d, ssd_norm_w,
          mla_q_norm_w, mla_w_q_up, mla_kv_norm_w, mla_w_kv_up, w_branch_out, w_out, post_norm_w):
    bsz, seqlen, _ = x.shape
    hn = rmsnorm(x, pre_norm_w)
    proj = hn @ w_in
    (xa, ga, bb, cb, xb, gb, xc, bc, cc, dtr, zc, ql, kvl, kpe_raw, gd, mg) = jnp.split(proj, _split_points(), axis=-1)

    xa_c, conv_a_new = causal_conv(xa, conv_a_prev, conv_a_w, conv_a_b)
    h_a, h_last = rglru(xa_c, h0, lru_w_r, lru_b_r, lru_w_i, lru_b_i, lru_lambda)
    y_a = h_a * jax.nn.silu(ga)

    v_b, conv_b_new = causal_conv(cb * xb, conv_b_prev, conv_b_w)
    y_b = bb * v_b * jax.nn.silu(gb)

    xbc, conv_c_new = causal_conv(jnp.concatenate([xc, bc, cc], axis=-1), conv_c_prev, conv_c_w, conv_c_b)
    xbc = jax.nn.silu(xbc)
    xs, bs, cs = jnp.split(xbc, [W_C, W_C + N_GROUPS_C * D_STATE], axis=-1)
    dt = jax.nn.softplus(dtr.astype(jnp.float32) + ssd_dt_bias.astype(jnp.float32))
    a_c = -jnp.exp(ssd_a_log.astype(jnp.float32))
    xs_h = xs.reshape(bsz, seqlen, N_HEADS_C, HEAD_DIM_C)
    y_ssd, ssd_new = ssd(xs_h, dt, a_c,
                         bs.reshape(bsz, seqlen, N_GROUPS_C, D_STATE),
                         cs.reshape(bsz, seqlen, N_GROUPS_C, D_STATE), ssd0)
    y_ssd = y_ssd + ssd_d.astype(jnp.float32)[:, None] * xs_h.astype(jnp.float32)
    y_c = gated_group_rmsnorm(y_ssd.reshape(bsz, seqlen, W_C), zc, ssd_norm_w)

    q = (rmsnorm(ql, mla_q_norm_w) @ mla_w_q_up).reshape(bsz, seqlen, N_HEADS_D, NOPE_DIM + ROPE_DIM)
    q_nope, q_pe = q[..., :NOPE_DIM], rope(q[..., NOPE_DIM:], pos)
    ckv = rmsnorm(kvl, mla_kv_norm_w)
    kpe = rope(kpe_raw, pos)
    if past_ckv is None:
        ckv_all, kpe_all = ckv, kpe
    else:
        ckv_all = jnp.concatenate([past_ckv.astype(ckv.dtype), ckv], axis=1)
        kpe_all = jnp.concatenate([past_kpe.astype(kpe.dtype), kpe], axis=1)
    nkeys = ckv_all.shape[1]
    kv = (ckv_all @ mla_w_kv_up).reshape(bsz, nkeys, N_HEADS_D, NOPE_DIM + V_DIM)
    k_nope, v_d = kv[..., :NOPE_DIM], kv[..., NOPE_DIM:]
    kpos = jnp.arange(nkeys)
    if past_ckv is None:
        o = attend_blocks(q_nope, q_pe, k_nope, kpe_all, v_d, kpos)
    else:
        o = attend(q_nope, q_pe, k_nope, kpe_all, v_d, pos, kpos)
    y_d = o.reshape(bsz, seqlen, W_D) * jax.nn.silu(gd)

    gates = jax.nn.sigmoid(mg.astype(jnp.float32)).astype(x.dtype).reshape(bsz, seqlen, N_BRANCH, D_MODEL)
    merged = gates[:, :, 0] * (y_a @ w_branch_out[0])
    for n, y_n in enumerate((y_b, y_c, y_d), start=1):
        merged = merged + gates[:, :, n] * (y_n @ w_branch_out[n])
    out = merged @ w_out
    x_new = x + rmsnorm(out, post_norm_w)
    return x_new, (ckv, kpe, conv_a_new, h_last, conv_b_new, conv_c_new, ssd_new)


def setup_inputs(seed: int = 0) -> dict:
    key = jax.random.key(seed)
    ks = jax.random.split(key, 40)
    f32 = jnp.float32

    def nrm(i, shape, scale):
        return scale * jax.random.normal(ks[i], shape, f32)

    def unif(i, shape, lo, hi):
        return jax.random.uniform(ks[i], shape, f32, lo, hi)

    a8 = unif(20, (DEPTH, W_A), 0.9, 0.999)
    sig = a8 ** (1.0 / LRU_C)
    lru_lambda = jnp.log(sig) - jnp.log1p(-sig)
    dt0 = jnp.exp(unif(21, (DEPTH, N_HEADS_C), float(np.log(1e-3)), float(np.log(1e-1))))
    ssd_dt_bias = dt0 + jnp.log(-jnp.expm1(-dt0))
    ssd_a_log = jnp.log(unif(22, (DEPTH, N_HEADS_C), 1.0, 16.0))
    return {
        'x_prompt': nrm(0, (BATCH, SEQ, D_MODEL), 1.0),
        'x_sample': nrm(1, (DEC_BATCH, DEC_SEQ, D_MODEL), 1.0),
        'cache_mla_latent': nrm(2, (DEPTH, DEC_BATCH, PAST_LEN, KV_LORA), 1.0),
        'cache_mla_kpe': nrm(3, (DEPTH, DEC_BATCH, PAST_LEN, ROPE_DIM), 1.0),
        'state_lru_conv': nrm(4, (DEPTH, DEC_BATCH, CONV_A - 1, W_A), 1.0),
        'state_lru_h': nrm(5, (DEPTH, DEC_BATCH, W_A), 0.5),
        'state_sconv': nrm(6, (DEPTH, DEC_BATCH, CONV_B - 1, W_B), 1.0),
        'state_ssd_conv': nrm(7, (DEPTH, DEC_BATCH, CONV_C - 1, CONV_DIM_C), 1.0),
        'state_ssd': nrm(8, (DEPTH, DEC_BATCH, N_HEADS_C, HEAD_DIM_C, D_STATE), 0.1),
        'pre_norm_w': 1.0 + nrm(9, (DEPTH, D_MODEL), 0.01),
        'w_in': nrm(10, (DEPTH, D_MODEL, D_IN), D_MODEL ** -0.5),
        'conv_a_w': nrm(11, (DEPTH, CONV_A, W_A), CONV_A ** -0.5),
        'conv_a_b': nrm(12, (DEPTH, W_A), 0.01),
        'lru_w_r': nrm(13, (DEPTH, LRU_BLOCKS, LRU_BLOCK, LRU_BLOCK), LRU_BLOCK ** -0.5),
        'lru_b_r': nrm(14, (DEPTH, W_A), 0.01),
        'lru_w_i': nrm(15, (DEPTH, LRU_BLOCKS, LRU_BLOCK, LRU_BLOCK), LRU_BLOCK ** -0.5),
        'lru_b_i': nrm(16, (DEPTH, W_A), 0.01),
        'lru_lambda': lru_lambda,
        'conv_b_w': nrm(17, (DEPTH, CONV_B, W_B), CONV_B ** -0.5),
        'conv_c_w': nrm(18, (DEPTH, CONV_C, CONV_DIM_C), CONV_C ** -0.5),
        'conv_c_b': nrm(19, (DEPTH, CONV_DIM_C), 0.01),
        'ssd_dt_bias': ssd_dt_bias,
        'ssd_a_log': ssd_a_log,
        'ssd_d': 1.0 + nrm(23, (DEPTH, N_HEADS_C), 0.01),
        'ssd_norm_w': 1.0 + nrm(24, (DEPTH, W_C), 0.01),
        'mla_q_norm_w': 1.0 + nrm(25, (DEPTH, Q_LORA), 0.01),
        'mla_w_q_up': nrm(26, (DEPTH, Q_LORA, N_HEADS_D * (NOPE_DIM + ROPE_DIM)), Q_LORA ** -0.5),
        'mla_kv_norm_w': 1.0 + nrm(27, (DEPTH, KV_LORA), 0.01),
        'mla_w_kv_up': nrm(28, (DEPTH, KV_LORA, N_HEADS_D * (NOPE_DIM + V_DIM)), KV_LORA ** -0.5),
        'w_branch_out': nrm(29, (DEPTH, N_BRANCH, BRANCH_W, D_MODEL), BRANCH_W ** -0.5),
        'w_out': nrm(30, (DEPTH, D_MODEL, D_MODEL), D_MODEL ** -0.5),
        'post_norm_w': 1.0 + nrm(31, (DEPTH, D_MODEL), 0.01),
    }


def reference(x_prompt, x_sample, cache_mla_latent, cache_mla_kpe, state_lru_conv, state_lru_h,
              state_sconv, state_ssd_conv, state_ssd, pre_norm_w, w_in, conv_a_w, conv_a_b,
              lru_w_r, lru_b_r, lru_w_i, lru_b_i, lru_lambda, conv_b_w, conv_c_w, conv_c_b,
              ssd_dt_bias, ssd_a_log, ssd_d, ssd_norm_w, mla_q_norm_w, mla_w_q_up, mla_kv_norm_w,
              mla_w_kv_up, w_branch_out, w_out, post_norm_w):
    bp, sp = x_prompt.shape[0], x_prompt.shape[1]
    past, ds = cache_mla_latent.shape[2], x_sample.shape[1]
    dtype = x_prompt.dtype
    pos_p = jnp.arange(sp)
    pos_s = past + jnp.arange(ds)
    z_conv_a = jnp.zeros((bp, CONV_A - 1, W_A), dtype)
    z_h = jnp.zeros((bp, W_A), dtype)
    z_conv_b = jnp.zeros((bp, CONV_B - 1, W_B), dtype)
    z_conv_c = jnp.zeros((bp, CONV_C - 1, CONV_DIM_C), dtype)
    z_ssd = jnp.zeros((bp, N_HEADS_C, HEAD_DIM_C, D_STATE), dtype)

    hp, hs = x_prompt, x_sample
    p_st = [[] for _ in range(7)]
    s_st = [[] for _ in range(7)]
    for l in range(DEPTH):
        lw = (pre_norm_w[l], w_in[l], conv_a_w[l], conv_a_b[l], lru_w_r[l], lru_b_r[l], lru_w_i[l],
              lru_b_i[l], lru_lambda[l], conv_b_w[l], conv_c_w[l], conv_c_b[l], ssd_dt_bias[l],
              ssd_a_log[l], ssd_d[l], ssd_norm_w[l], mla_q_norm_w[l], mla_w_q_up[l], mla_kv_norm_w[l],
              mla_w_kv_up[l], w_branch_out[l], w_out[l], post_norm_w[l])
        hp, new_p = layer(hp, pos_p, z_conv_a, z_h, z_conv_b, z_conv_c, z_ssd, None, None, *lw)
        hs, new_s = layer(hs, pos_s, state_lru_conv[l], state_lru_h[l], state_sconv[l],
                          state_ssd_conv[l], state_ssd[l], cache_mla_latent[l], cache_mla_kpe[l], *lw)
        for k in range(7):
            p_st[k].append(new_p[k])
            s_st[k].append(new_s[k])
    P = [jnp.stack(v, axis=0) for v in p_st]
    S = [jnp.stack(v, axis=0) for v in s_st]
    return (hp, hs, P[0], P[1], P[2], P[3], P[4], P[5], P[6], S[0], S[1], S[2], S[3], S[4], S[5], S[6])
```

```python
import functools

import jax
import jax.numpy as jnp
from jax import lax
from jax.experimental import pallas as pl
from jax.experimental.pallas import tpu as pltpu

F32 = jnp.float32
BF16 = jnp.bfloat16

D_MODEL = 4096
DEPTH = 4
CHUNK = 64
BRANCH_W = 1024
N_BRANCH = 4
EPS = 1e-6
LRU_BLOCKS = 8
LRU_BLOCK = 128
LRU_C = 8.0
N_HEADS_C = 16
HEAD_DIM_C = 64
N_GROUPS_C = 2
D_STATE = 128
N_HEADS_D = 8
NOPE_DIM = 128
ROPE_DIM = 64
V_DIM = 128
Q_LORA = 1024
KV_LORA = 512
ROPE_THETA = 10000.0
ATTN_SCALE = (NOPE_DIM + ROPE_DIM) ** -0.5
HALF = ROPE_DIM // 2

O_BC = 7168
O_DTR = 7680
O_ZC = 7696
O_QL = 8720
O_KVL = 9744
O_KPE = 10256
O_GD = 10320
O_MG = 11344

C_XA, C_GA, C_BB, C_CB, C_XB, C_GB, C_XC, C_ZC, C_QL, C_GD = (i * 1024 for i in range(10))
C_BCC = 10240
C_KVL = 10752
C_KD = 11264
C_KSW = 11392
PACK_W = 11520
DT_LANE = 64
Q_SLAB = 256

LANE = 128
SUBLANE = 8
VMEM_LIMIT = 56 * 1024 * 1024

NT_DIMS = (((1,), (1,)), ((), ()))
HI = lax.Precision.HIGHEST


def _pick(n, pref):
    if n <= pref:
        return n
    t = pref
    while n % t:
        t -= SUBLANE
    return t


def _params(*sem):
    return pltpu.CompilerParams(dimension_semantics=sem, vmem_limit_bytes=VMEM_LIMIT)


def _full(shape):
    zeros = (0,) * len(shape)
    return pl.BlockSpec(shape, lambda *_: zeros)


def _sigmoid(x):
    return 1.0 / (1.0 + jnp.exp(-x))


def _silu(x):
    return x * _sigmoid(x)


def _softplus(x):
    return jnp.maximum(x, 0.0) + jnp.log1p(jnp.exp(-jnp.abs(x)))


def _neg_expm1(y):
    e = jnp.exp(y)
    near = jnp.where(e == 1.0, -y, (1.0 - e) * y / jnp.log(e))
    return jnp.where(y > -0.5, near, 1.0 - e)


def _rms(x, w):
    return x * lax.rsqrt(jnp.mean(x * x, axis=-1, keepdims=True) + EPS) * w


def _bdot(a, b):
    return jnp.dot(a, b, preferred_element_type=F32)


def _rms_cast_kernel(x_ref, w_ref, o_ref):
    o_ref[...] = _rms(x_ref[...], w_ref[...]).astype(o_ref.dtype)


def _rms_cast(x, w):
    m, d = x.shape
    tm = _pick(m, 256)
    return pl.pallas_call(
        _rms_cast_kernel,
        grid=(m // tm,),
        in_specs=[pl.BlockSpec((tm, d), lambda i: (i, 0)), _full((1, d))],
        out_specs=pl.BlockSpec((tm, d), lambda i: (i, 0)),
        out_shape=jax.ShapeDtypeStruct((m, d), BF16),
        compiler_params=_params("parallel"),
        name="rms_cast",
    )(x, w)


def _mm_kernel(a_ref, w_ref, o_ref):
    o_ref[...] = _bdot(a_ref[...], w_ref[...]).astype(o_ref.dtype)


def _matmul(a, w, tn, out_dtype, name):
    m, k = a.shape
    n = w.shape[1]
    tm = _pick(m, 1024)
    return pl.pallas_call(
        _mm_kernel,
        grid=(m // tm, n // tn),
        in_specs=[pl.BlockSpec((tm, k), lambda i, j: (i, 0)), pl.BlockSpec((k, tn), lambda i, j: (0, j))],
        out_specs=pl.BlockSpec((tm, tn), lambda i, j: (i, j)),
        out_shape=jax.ShapeDtypeStruct((m, n), out_dtype),
        compiler_params=_params("parallel", "arbitrary"),
        name=name,
    )(a, w)


def _merge_kernel(h_ref, ya_ref, yb_ref, yc_ref, yd_ref, g0, g1, g2, g3, b0, b1, b2, b3, o_ref):
    h = h_ref[...]
    acc = None
    for y_ref, g_ref, b_ref in ((ya_ref, g0, b0), (yb_ref, g1, b1), (yc_ref, g2, b2), (yd_ref, g3, b3)):
        term = _sigmoid(_bdot(h, g_ref[...])) * _bdot(y_ref[...], b_ref[...])
        acc = term if acc is None else acc + term
    o_ref[...] = acc.astype(o_ref.dtype)


def _merge(h, ys, wmg, wb):
    m = h.shape[0]
    tm = _pick(m, 512)
    tn = 256
    nj = D_MODEL // tn
    y_spec = pl.BlockSpec((tm, BRANCH_W), lambda i, j: (i, 0))
    g_specs = [pl.BlockSpec((D_MODEL, tn), functools.partial(lambda i, j, n: (0, n * nj + j), n=n)) for n in range(N_BRANCH)]
    b_specs = [pl.BlockSpec((None, BRANCH_W, tn), functools.partial(lambda i, j, n: (n, 0, j), n=n)) for n in range(N_BRANCH)]
    return pl.pallas_call(
        _merge_kernel,
        grid=(m // tm, nj),
        in_specs=[pl.BlockSpec((tm, D_MODEL), lambda i, j: (i, 0))] + [y_spec] * 4 + g_specs + b_specs,
        out_specs=pl.BlockSpec((tm, tn), lambda i, j: (i, j)),
        out_shape=jax.ShapeDtypeStruct((m, D_MODEL), BF16),
        compiler_params=_params("parallel", "arbitrary"),
        name="gate_merge",
    )(h, *ys, wmg, wmg, wmg, wmg, wb, wb, wb, wb)


def _post_kernel(x_ref, o_ref, pw_ref, nw_ref, xn_ref, hn_ref):
    xn = x_ref[...] + _rms(o_ref[...], pw_ref[...])
    xn_ref[...] = xn
    hn_ref[...] = _rms(xn, nw_ref[...]).astype(hn_ref.dtype)


def _post_last_kernel(x_ref, o_ref, pw_ref, xn_ref):
    xn_ref[...] = x_ref[...] + _rms(o_ref[...], pw_ref[...])


def _post(x, out, post_w, next_pre_w):
    m, d = x.shape
    tm = _pick(m, 256)
    row = pl.BlockSpec((tm, d), lambda i: (i, 0))
    if next_pre_w is None:
        return pl.pallas_call(
            _post_last_kernel, grid=(m // tm,), in_specs=[row, row, _full((1, d))], out_specs=row,
            out_shape=jax.ShapeDtypeStruct((m, d), F32), compiler_params=_params("parallel"), name="post_last",
        )(x, out, post_w), None
    return pl.pallas_call(
        _post_kernel, grid=(m // tm,), in_specs=[row, row, _full((1, d)), _full((1, d))], out_specs=[row, row],
        out_shape=[jax.ShapeDtypeStruct((m, d), F32), jax.ShapeDtypeStruct((m, d), BF16)],
        compiler_params=_params("parallel"), name="post",
    )(x, out, post_w, next_pre_w)


def _conv_tile(ext_s, x, w_ref, width, tl):
    ext_s[8:8 + tl, :] = x
    k0 = 9 - width
    y = ext_s[k0:k0 + tl, :] * w_ref[0:1, :]
    for j in range(1, width - 1):
        y = y + ext_s[k0 + j:k0 + j + tl, :] * w_ref[j:j + 1, :]
    y = y + x * w_ref[width - 1:width, :]
    return y


def _conv_carry(ext_s, width, tl):
    tail = ext_s[tl + 9 - width:tl + 8, :]
    ext_s[9 - width:8, :] = tail
    return tail


def _lru_kernel(xa_ref, ga_ref, cprev_ref, h0_ref, cw_ref, cb_ref, wr_ref, br_ref, wi_ref, bi_ref, lam_ref,
                y_ref, hlast_ref, cnew_ref, ext_s, a_s, u_s, h_s, *, tl, nl):
    l = pl.program_id(1)

    @pl.when(l == 0)
    def _():
        ext_s[5:8, :] = cprev_ref[...]
        h_s[...] = h0_ref[...]

    xc = _conv_tile(ext_s, xa_ref[...], cw_ref, 4, tl) + cb_ref[...]
    tail = _conv_carry(ext_s, 4, tl)

    xcb = xc.astype(BF16)
    rs, gs = [], []
    for n in range(LRU_BLOCKS):
        blk = xcb[:, n * LRU_BLOCK:(n + 1) * LRU_BLOCK]
        rs.append(_bdot(blk, wr_ref[n]))
        gs.append(_bdot(blk, wi_ref[n]))
    r = _sigmoid(jnp.concatenate(rs, axis=1) + br_ref[...])
    gate_i = _sigmoid(jnp.concatenate(gs, axis=1) + bi_ref[...])
    log_a = (-LRU_C * r) * _softplus(-lam_ref[...])
    a_s[...] = jnp.exp(log_a)
    u_s[...] = jnp.sqrt(_neg_expm1(2.0 * log_a)) * (gate_i * xc)

    def step(t, h):
        h = a_s[pl.ds(t, 1), :] * h + u_s[pl.ds(t, 1), :]
        u_s[pl.ds(t, 1), :] = h
        return h

    h = lax.fori_loop(0, tl, step, h_s[...], unroll=8)
    h_s[...] = h
    y_ref[...] = (u_s[...] * _silu(ga_ref[...])).astype(y_ref.dtype)

    @pl.when(l == nl - 1)
    def _():
        hlast_ref[...] = h
        cnew_ref[...] = tail


def _lru(p, cprev, h0, w):
    b, seq, _ = p.shape
    tl = _pick(seq, 256)
    nl = seq // tl
    col = lambda c: pl.BlockSpec((None, tl, BRANCH_W), lambda i, l: (i, l, c // BRANCH_W))
    st3 = pl.BlockSpec((None, 3, BRANCH_W), lambda i, l: (i, 0, 0))
    st1 = pl.BlockSpec((None, 1, BRANCH_W), lambda i, l: (i, 0, 0))
    vec = _full((1, BRANCH_W))
    blk = _full((LRU_BLOCKS, LRU_BLOCK, LRU_BLOCK))
    return pl.pallas_call(
        functools.partial(_lru_kernel, tl=tl, nl=nl),
        grid=(b, nl),
        in_specs=[col(C_XA), col(C_GA), st3, st1, _full((4, BRANCH_W)), vec, blk, vec, blk, vec, vec],
        out_specs=[pl.BlockSpec((None, tl, BRANCH_W), lambda i, l: (i, l, 0)), st1, st3],
        out_shape=[jax.ShapeDtypeStruct((b, seq, BRANCH_W), BF16), jax.ShapeDtypeStruct((b, 1, BRANCH_W), F32),
                   jax.ShapeDtypeStruct((b, 3, BRANCH_W), F32)],
        scratch_shapes=[pltpu.VMEM((tl + 8, BRANCH_W), F32), pltpu.VMEM((tl, BRANCH_W), F32),
                        pltpu.VMEM((tl, BRANCH_W), F32), pltpu.VMEM((1, BRANCH_W), F32)],
        compiler_params=_params("parallel", "arbitrary"),
        name="rglru",
    )(p, p, cprev, h0, w["conv_a_w"], w["conv_a_b"], w["lru_w_r"], w["lru_b_r"], w["lru_w_i"], w["lru_b_i"], w["lru_lambda"])


def _sconv_kernel(bb_ref, cb_ref, xb_ref, gb_ref, cprev_ref, cw_ref, y_ref, cnew_ref, ext_s, *, tl, nl):
    l = pl.program_id(1)

    @pl.when(l == 0)
    def _():
        ext_s[6:8, :] = cprev_ref[...]

    v = _conv_tile(ext_s, cb_ref[...] * xb_ref[...], cw_ref, 3, tl)
    tail = _conv_carry(ext_s, 3, tl)
    y_ref[...] = (bb_ref[...] * v * _silu(gb_ref[...])).astype(y_ref.dtype)

    @pl.when(l == nl - 1)
    def _():
        cnew_ref[...] = tail


def _sconv(p, cprev, w):
    b, seq, _ = p.shape
    tl = _pick(seq, 256)
    nl = seq // tl
    col = lambda c: pl.BlockSpec((None, tl, BRANCH_W), lambda i, l: (i, l, c // BRANCH_W))
    st2 = pl.BlockSpec((None, 2, BRANCH_W), lambda i, l: (i, 0, 0))
    return pl.pallas_call(
        functools.partial(_sconv_kernel, tl=tl, nl=nl),
        grid=(b, nl),
        in_specs=[col(C_BB), col(C_CB), col(C_XB), col(C_GB), st2, _full((3, BRANCH_W))],
        out_specs=[pl.BlockSpec((None, tl, BRANCH_W), lambda i, l: (i, l, 0)), st2],
        out_shape=[jax.ShapeDtypeStruct((b, seq, BRANCH_W), BF16), jax.ShapeDtypeStruct((b, 2, BRANCH_W), F32)],
        scratch_shapes=[pltpu.VMEM((tl + 8, BRANCH_W), F32)],
        compiler_params=_params("parallel", "arbitrary"),
        name="sconv",
    )(p, p, p, p, cprev, w["conv_b_w"])


def _ssd_kernel(xc_ref, bcc_ref, kd_ref, zc_ref, cpx_ref, cpb_ref, st0_ref, cwx_ref, cbx_ref, cwb_ref, cbb_ref,
                dtb_ref, alog_ref, dexp_ref, nw_ref, e_ref,
                y_ref, cnx_ref, cnb_ref, st_ref, extx_s, extb_s, st_s, *, q, nl):
    l = pl.program_id(1)

    @pl.when(l == 0)
    def _():
        extx_s[5:8, :] = cpx_ref[...]
        extb_s[5:8, :] = cpb_ref[...]
        st_s[...] = st0_ref[...]

    xs = _silu(_conv_tile(extx_s, xc_ref[...], cwx_ref, 4, q) + cbx_ref[...])
    tailx = _conv_carry(extx_s, 4, q)
    bcs = _silu(_conv_tile(extb_s, bcc_ref[...], cwb_ref, 4, q) + cbb_ref[...])
    tailb = _conv_carry(extb_s, 4, q)

    dt = _softplus(kd_ref[...] + dtb_ref[...])
    d_a = dt * (-jnp.exp(alog_ref[...]))
    rows = lax.broadcasted_iota(jnp.int32, (q, q), 0)
    cols = lax.broadcasted_iota(jnp.int32, (q, q), 1)
    tri = rows >= cols
    acum = jnp.dot(tri.astype(F32), d_a, precision=HI, preferred_element_type=F32)
    eye = (lax.broadcasted_iota(jnp.int32, (LANE, LANE), 0) == lax.broadcasted_iota(jnp.int32, (LANE, LANE), 1))
    acum_t = lax.dot_general(eye.astype(F32), acum, NT_DIMS, precision=HI, preferred_element_type=F32)
    expand = e_ref[...]
    dt_x = jnp.dot(dt, expand, precision=HI, preferred_element_type=F32)
    acum_x = jnp.dot(acum, expand, precision=HI, preferred_element_type=F32)
    end_x = acum_x[q - 1:q, :]
    xdt = xs * dt_x
    xw = (xdt * jnp.exp(end_x - acum_x)).astype(BF16)
    xdt_b = xdt.astype(BF16)
    eacum_x = jnp.exp(acum_x)
    cdecay_x = jnp.exp(end_x)
    eye_b = eye.astype(BF16)

    y_parts = []
    for g in range(N_GROUPS_C):
        bg = bcs[:, g * D_STATE:(g + 1) * D_STATE].astype(BF16)
        cg = bcs[:, (N_GROUPS_C + g) * D_STATE:(N_GROUPS_C + g + 1) * D_STATE].astype(BF16)
        cb = lax.dot_general(cg, bg, NT_DIMS, preferred_element_type=F32)
        bg_t = lax.dot_general(eye_b, bg, NT_DIMS, preferred_element_type=F32).astype(BF16)
        for e in range(N_HEADS_C // N_GROUPS_C):
            h = g * (N_HEADS_C // N_GROUPS_C) + e
            lo = h * HEAD_DIM_C
            seg = acum_x[:, lo:lo + q] - acum_t[DT_LANE + h:DT_LANE + h + 1, :]
            decay = jnp.exp(jnp.where(tri, seg, -jnp.inf))
            y_diag = _bdot((cb * decay).astype(BF16), xdt_b[:, lo:lo + HEAD_DIM_C])
            prev = st_s[h]
            y_off = _bdot(cg, prev.astype(BF16)) * eacum_x[:, lo:lo + HEAD_DIM_C]
            st_s[h] = prev * cdecay_x[:, lo:lo + HEAD_DIM_C] + _bdot(bg_t, xw[:, lo:lo + HEAD_DIM_C])
            y_parts.append(y_diag + y_off)
    y = jnp.concatenate(y_parts, axis=1) + dexp_ref[...] * xs

    gated = y * _silu(zc_ref[...])
    half = BRANCH_W // N_GROUPS_C
    normed = jnp.concatenate(
        [gated[:, i * half:(i + 1) * half]
         * lax.rsqrt(jnp.mean(gated[:, i * half:(i + 1) * half] ** 2, axis=-1, keepdims=True) + EPS)
         for i in range(N_GROUPS_C)], axis=1)
    y_ref[...] = (normed * nw_ref[...]).astype(y_ref.dtype)

    @pl.when(l == nl - 1)
    def _():
        cnx_ref[...] = tailx
        cnb_ref[...] = tailb
        st_ref[...] = st_s[...]


def _ssd(p, cprev_x, cprev_b, st0, w):
    b, seq, _ = p.shape
    q = CHUNK if seq % CHUNK == 0 else seq
    nl = seq // q
    bcw = 2 * N_GROUPS_C * D_STATE
    st_shape = (N_HEADS_C, D_STATE, HEAD_DIM_C)
    colx = lambda c, wd: pl.BlockSpec((None, q, wd), lambda i, l: (i, l, c // wd))
    st3 = lambda wd: pl.BlockSpec((None, 3, wd), lambda i, l: (i, 0, 0))
    st_spec = pl.BlockSpec((None,) + st_shape, lambda i, l: (i, 0, 0, 0))
    return pl.pallas_call(
        functools.partial(_ssd_kernel, q=q, nl=nl),
        grid=(b, nl),
        in_specs=[colx(C_XC, BRANCH_W), colx(C_BCC, bcw), colx(C_KD, LANE), colx(C_ZC, BRANCH_W),
                  st3(BRANCH_W), st3(bcw), st_spec,
                  _full((4, BRANCH_W)), _full((1, BRANCH_W)), _full((4, bcw)), _full((1, bcw)),
                  _full((1, LANE)), _full((1, LANE)), _full((1, BRANCH_W)), _full((1, BRANCH_W)), _full((LANE, BRANCH_W))],
        out_specs=[pl.BlockSpec((None, q, BRANCH_W), lambda i, l: (i, l, 0)), st3(BRANCH_W), st3(bcw), st_spec],
        out_shape=[jax.ShapeDtypeStruct((b, seq, BRANCH_W), BF16), jax.ShapeDtypeStruct((b, 3, BRANCH_W), F32),
                   jax.ShapeDtypeStruct((b, 3, bcw), F32), jax.ShapeDtypeStruct((b,) + st_shape, F32)],
        scratch_shapes=[pltpu.VMEM((q + 8, BRANCH_W), F32), pltpu.VMEM((q + 8, bcw), F32), pltpu.VMEM(st_shape, F32)],
        compiler_params=_params("parallel", "arbitrary"),
        name="ssd",
    )(p, p, p, p, cprev_x, cprev_b, st0, w["conv_cx_w"], w["conv_cx_b"], w["conv_cb_w"], w["conv_cb_b"],
      w["dt_bias"], w["a_log"], w["d_exp"], w["ssd_norm_w"], w["expand"])


def _mla_prep_kernel(ql_ref, kvl_ref, kd_ref, ksw_ref, cq_ref, sq_ref, ck_ref, sk_ref, qnw_ref, kvnw_ref, wq_ref, wqs_ref,
                     q_ref, ckv_ref, kpe_ref, kpe128_ref):
    qn = _rms(ql_ref[...], qnw_ref[...]).astype(BF16)
    q = _bdot(qn, wq_ref[...])
    q_sw = _bdot(qn, wqs_ref[...])
    cos_q, sin_q = cq_ref[...], sq_ref[...]
    for h in range(N_HEADS_D):
        sl = slice(h * Q_SLAB, (h + 1) * Q_SLAB)
        q_ref[:, sl] = (q[:, sl] * cos_q + q_sw[:, sl] * sin_q).astype(q_ref.dtype)
    ckv_ref[...] = _rms(kvl_ref[...], kvnw_ref[...])
    kr = kd_ref[...] * ck_ref[...] + ksw_ref[...] * sk_ref[...]
    kpe128_ref[...] = kr
    kpe_ref[...] = kr[:, :ROPE_DIM]


def _mla_prep(p2, tabs, w):
    m = p2.shape[0]
    tm = _pick(m, 256)
    col = lambda c, wd: pl.BlockSpec((tm, wd), lambda i: (i, c // wd))
    row = lambda wd: pl.BlockSpec((tm, wd), lambda i: (i, 0))
    qw = N_HEADS_D * Q_SLAB
    return pl.pallas_call(
        _mla_prep_kernel,
        grid=(m // tm,),
        in_specs=[col(C_QL, Q_LORA), col(C_KVL, KV_LORA), col(C_KD, LANE), col(C_KSW, LANE),
                  row(Q_SLAB), row(Q_SLAB), row(LANE), row(LANE),
                  _full((1, Q_LORA)), _full((1, KV_LORA)), _full((Q_LORA, qw)), _full((Q_LORA, qw))],
        out_specs=[row(qw), row(KV_LORA), row(ROPE_DIM), row(LANE)],
        out_shape=[jax.ShapeDtypeStruct((m, qw), BF16), jax.ShapeDtypeStruct((m, KV_LORA), F32),
                   jax.ShapeDtypeStruct((m, ROPE_DIM), F32), jax.ShapeDtypeStruct((m, LANE), F32)],
        compiler_params=_params("parallel"),
        name="mla_prep",
    )(p2, p2, p2, p2, tabs["cos_q"], tabs["sin_q"], tabs["cos_k"], tabs["sin_k"],
      w["mla_q_norm_w"], w["mla_kv_norm_w"], w["wq"], w["wq_sw"])


def _kv_expand_kernel(ckv_ref, kpe_ref, wk_ref, wv_ref, k_ref, v_ref):
    c = ckv_ref[...].astype(BF16)
    kn = _bdot(c, wk_ref[...])
    v_ref[...] = _bdot(c, wv_ref[...]).astype(v_ref.dtype)
    kr = kpe_ref[...]
    for h in range(N_HEADS_D):
        k_ref[:, h * Q_SLAB:h * Q_SLAB + NOPE_DIM] = kn[:, h * NOPE_DIM:(h + 1) * NOPE_DIM].astype(k_ref.dtype)
        k_ref[:, h * Q_SLAB + NOPE_DIM:(h + 1) * Q_SLAB] = kr.astype(k_ref.dtype)


def _kv_expand(ckv, kpe128, w):
    r = ckv.shape[0]
    tr = _pick(r, 512)
    row = lambda wd: pl.BlockSpec((tr, wd), lambda i: (i, 0))
    kw, vw = N_HEADS_D * Q_SLAB, N_HEADS_D * V_DIM
    return pl.pallas_call(
        _kv_expand_kernel,
        grid=(r // tr,),
        in_specs=[row(KV_LORA), row(LANE), _full((KV_LORA, N_HEADS_D * NOPE_DIM)), _full((KV_LORA, vw))],
        out_specs=[row(kw), row(vw)],
        out_shape=[jax.ShapeDtypeStruct((r, kw), BF16), jax.ShapeDtypeStruct((r, vw), BF16)],
        compiler_params=_params("parallel"),
        name="kv_expand",
    )(ckv, kpe128, w["wk"], w["wv"])


def _attn_kernel(q_ref, k_ref, v_ref, gd_ref, o_ref, m_s, l_s, acc_s, *, tq, tk, nk, q_off):
    i = pl.program_id(2)
    j = pl.program_id(3)

    @pl.when(j == 0)
    def _():
        m_s[...] = jnp.full(m_s.shape, -jnp.inf, F32)
        l_s[...] = jnp.zeros(l_s.shape, F32)
        acc_s[...] = jnp.zeros(acc_s.shape, F32)

    q_lo = q_off + i * tq
    k_lo = j * tk
    last_bound = ((q_lo + tq - 1) // CHUNK + 1) * CHUNK
    first_bound = (q_lo // CHUNK + 1) * CHUNK

    def block(masked):
        s = lax.dot_general(q_ref[...], k_ref[...], NT_DIMS, preferred_element_type=F32) * ATTN_SCALE
        if masked:
            qpos = q_lo + lax.broadcasted_iota(jnp.int32, (tq, tk), 0)
            kpos = k_lo + lax.broadcasted_iota(jnp.int32, (tq, tk), 1)
            s = jnp.where(kpos < (qpos // CHUNK + 1) * CHUNK, s, -1e30)
        m_prev = m_s[...]
        m_next = jnp.maximum(m_prev, jnp.max(s, axis=1, keepdims=True))
        alpha = jnp.exp(m_prev - m_next)
        pr = jnp.exp(s - m_next[:, 0:1])
        l_s[...] = alpha * l_s[...] + jnp.sum(pr, axis=1, keepdims=True)
        m_s[...] = m_next
        acc_s[...] = acc_s[...] * alpha + _bdot(pr.astype(BF16), v_ref[...])

    @pl.when(k_lo + tk <= first_bound)
    def _():
        block(False)

    @pl.when(jnp.logical_and(k_lo + tk > first_bound, k_lo < last_bound))
    def _():
        block(True)

    @pl.when(j == nk - 1)
    def _():
        o_ref[...] = (acc_s[...] / l_s[...] * _silu(gd_ref[...])).astype(o_ref.dtype)


def _attention(q, k, v, p, q_off):
    b, lq, _ = q.shape
    lk = k.shape[1]
    tq = _pick(lq, 512)
    tk = _pick(lk, 512) if lk % 512 == 0 else lk
    nq, nk = lq // tq, lk // tk
    last_j = lambda i: (q_off + (i + 1) * tq - 1) // tk
    kv_map = lambda bi, h, i, j: (bi, jnp.minimum(j, last_j(i)), h)
    return pl.pallas_call(
        functools.partial(_attn_kernel, tq=tq, tk=tk, nk=nk, q_off=q_off),
        grid=(b, N_HEADS_D, nq, nk),
        in_specs=[pl.BlockSpec((None, tq, Q_SLAB), lambda bi, h, i, j: (bi, i, h)),
                  pl.BlockSpec((None, tk, Q_SLAB), kv_map),
                  pl.BlockSpec((None, tk, V_DIM), kv_map),
                  pl.BlockSpec((None, tq, V_DIM), lambda bi, h, i, j: (bi, i, C_GD // V_DIM + h))],
        out_specs=pl.BlockSpec((None, tq, V_DIM), lambda bi, h, i, j: (bi, i, h)),
        out_shape=jax.ShapeDtypeStruct((b, lq, N_HEADS_D * V_DIM), BF16),
        scratch_shapes=[pltpu.VMEM((tq, LANE), F32), pltpu.VMEM((tq, LANE), F32), pltpu.VMEM((tq, V_DIM), F32)],
        compiler_params=_params("parallel", "parallel", "parallel", "arbitrary"),
        name="mla_attention",
    )(q, k, v, p)


def _prep_weights(pre_norm_w, w_in, conv_a_w, conv_a_b, lru_w_r, lru_b_r, lru_w_i, lru_b_i, lru_lambda, conv_b_w,
                  conv_c_w, conv_c_b, ssd_dt_bias, ssd_a_log, ssd_d, ssd_norm_w, mla_q_norm_w, mla_w_q_up,
                  mla_kv_norm_w, mla_w_kv_up, w_branch_out, w_out, post_norm_w):
    d = w_in.shape[0]
    zin = lambda n: jnp.zeros((d, D_MODEL, n), w_in.dtype)
    k1 = w_in[..., O_KPE:O_KPE + HALF]
    k2 = w_in[..., O_KPE + HALF:O_KPE + ROPE_DIM]
    w_pack = jnp.concatenate(
        [w_in[..., :O_BC], w_in[..., O_ZC:O_ZC + 1024], w_in[..., O_QL:O_QL + 1024], w_in[..., O_GD:O_GD + 1024],
         w_in[..., O_BC:O_DTR], w_in[..., O_KVL:O_KVL + KV_LORA],
         k1, k2, w_in[..., O_DTR:O_DTR + N_HEADS_C], zin(LANE - ROPE_DIM - N_HEADS_C),
         -k2, k1, zin(LANE - ROPE_DIM)], axis=-1).astype(BF16)
    w_mg = w_in[..., O_MG:].astype(BF16)

    wq = mla_w_q_up.reshape(d, Q_LORA, N_HEADS_D, NOPE_DIM + ROPE_DIM)
    nope, x1, x2 = wq[..., :NOPE_DIM], wq[..., NOPE_DIM:NOPE_DIM + HALF], wq[..., NOPE_DIM + HALF:]
    zq = lambda n: jnp.zeros((d, Q_LORA, N_HEADS_D, n), wq.dtype)
    wq_main = jnp.concatenate([nope, x1, x2, zq(Q_SLAB - NOPE_DIM - ROPE_DIM)], -1).reshape(d, Q_LORA, -1).astype(BF16)
    wq_sw = jnp.concatenate([zq(NOPE_DIM), -x2, x1, zq(Q_SLAB - NOPE_DIM - ROPE_DIM)], -1).reshape(d, Q_LORA, -1).astype(BF16)
    wkv = mla_w_kv_up.reshape(d, KV_LORA, N_HEADS_D, NOPE_DIM + V_DIM)
    wk = wkv[..., :NOPE_DIM].reshape(d, KV_LORA, -1).astype(BF16)
    wv = wkv[..., NOPE_DIM:].reshape(d, KV_LORA, -1).astype(BF16)

    pad_heads = lambda a: jnp.pad(a, ((0, 0), (DT_LANE, LANE - DT_LANE - N_HEADS_C)))[:, None, :]
    lanes = jnp.arange(BRANCH_W) // HEAD_DIM_C + DT_LANE
    expand = (jnp.arange(LANE)[:, None] == lanes[None, :]).astype(F32)
    vec = lambda a: a[:, None, :]
    return dict(
        pre_norm_w=vec(pre_norm_w), post_norm_w=vec(post_norm_w), w_pack=w_pack, w_mg=w_mg,
        w_branch=w_branch_out.astype(BF16), w_out=w_out.astype(BF16),
        conv_a_w=conv_a_w, conv_a_b=vec(conv_a_b), lru_w_r=lru_w_r.astype(BF16), lru_b_r=vec(lru_b_r),
        lru_w_i=lru_w_i.astype(BF16), lru_b_i=vec(lru_b_i), lru_lambda=vec(lru_lambda), conv_b_w=conv_b_w,
        conv_cx_w=conv_c_w[..., :BRANCH_W], conv_cx_b=vec(conv_c_b[..., :BRANCH_W]),
        conv_cb_w=conv_c_w[..., BRANCH_W:], conv_cb_b=vec(conv_c_b[..., BRANCH_W:]),
        dt_bias=pad_heads(ssd_dt_bias), a_log=pad_heads(ssd_a_log), d_exp=vec(jnp.repeat(ssd_d, HEAD_DIM_C, axis=-1)),
        ssd_norm_w=vec(ssd_norm_w), expand=jnp.broadcast_to(expand, (d,) + expand.shape),
        mla_q_norm_w=vec(mla_q_norm_w), mla_kv_norm_w=vec(mla_kv_norm_w), wq=wq_main, wq_sw=wq_sw, wk=wk, wv=wv,
    )


def _rope_tables(pos, reps):
    inv = ROPE_THETA ** (-jnp.arange(HALF, dtype=F32) / HALF)
    ang = pos.astype(F32)[:, None] * inv[None, :]
    cos, sin = jnp.cos(ang), jnp.sin(ang)
    n = pos.shape[0]
    z = lambda w: jnp.zeros((n, w), F32)
    tabs = dict(
        cos_q=jnp.concatenate([jnp.ones((n, NOPE_DIM), F32), cos, cos, z(Q_SLAB - NOPE_DIM - ROPE_DIM)], 1),
        sin_q=jnp.concatenate([z(NOPE_DIM), sin, sin, z(Q_SLAB - NOPE_DIM - ROPE_DIM)], 1),
        cos_k=jnp.concatenate([cos, cos, z(LANE - ROPE_DIM)], 1),
        sin_k=jnp.concatenate([sin, sin, z(LANE - ROPE_DIM)], 1),
    )
    return {k: jnp.tile(v, (reps, 1)) for k, v in tabs.items()}


def _layer(x, h, b, seq, tabs, q_off, state, past, w, next_pre_w):
    conv_a_prev, h0, conv_b_prev, conv_c_prev, ssd0 = state
    p2 = _matmul(h, w["w_pack"], 1280, F32, "in_proj")
    p3 = p2.reshape(b, seq, PACK_W)

    y_a, h_last, conv_a_new = _lru(p3, conv_a_prev, h0[:, None, :], w)
    y_b, conv_b_new = _sconv(p3, conv_b_prev, w)
    y_c, conv_cx_new, conv_cb_new, ssd_t = _ssd(p3, conv_c_prev[..., :BRANCH_W], conv_c_prev[..., BRANCH_W:],
                                                 jnp.swapaxes(ssd0, -1, -2), w)
    q, ckv, kpe, kpe128 = _mla_prep(p2, tabs, w)
    if past is None:
        ckv_all, kpe_all, lk = ckv, kpe128, seq
    else:
        past_ckv, past_kpe = past
        lk = past_ckv.shape[1] + seq
        ckv_all = jnp.concatenate([past_ckv, ckv.reshape(b, seq, KV_LORA)], axis=1).reshape(b * lk, KV_LORA)
        past_kpe128 = jnp.pad(past_kpe, ((0, 0), (0, 0), (0, LANE - ROPE_DIM)))
        kpe_all = jnp.concatenate([past_kpe128, kpe128.reshape(b, seq, LANE)], axis=1).reshape(b * lk, LANE)
    k, v = _kv_expand(ckv_all, kpe_all, w)
    y_d = _attention(q.reshape(b, seq, -1), k.reshape(b, lk, -1), v.reshape(b, lk, -1), p3, q_off)

    m = b * seq
    merged = _merge(h, [y.reshape(m, BRANCH_W) for y in (y_a, y_b, y_c, y_d)], w["w_mg"], w["w_branch"])
    out = _matmul(merged, w["w_out"], 1024, F32, "out_proj")
    x_new, h_next = _post(x, out, w["post_norm_w"], next_pre_w)
    new_state = (ckv.reshape(b, seq, KV_LORA), kpe.reshape(b, seq, ROPE_DIM), conv_a_new, h_last[:, 0, :], conv_b_new,
                 jnp.concatenate([conv_cx_new, conv_cb_new], axis=-1), jnp.swapaxes(ssd_t, -1, -2))
    return x_new, h_next, new_state


def kernel(x_prompt, x_sample, cache_mla_latent, cache_mla_kpe, state_lru_conv, state_lru_h, state_sconv, state_ssd_conv, state_ssd, pre_norm_w, w_in, conv_a_w, conv_a_b, lru_w_r, lru_b_r, lru_w_i, lru_b_i, lru_lambda, conv_b_w, conv_c_w, conv_c_b, ssd_dt_bias, ssd_a_log, ssd_d, ssd_norm_w, mla_q_norm_w, mla_w_q_up, mla_kv_norm_w, mla_w_kv_up, w_branch_out, w_out, post_norm_w):
    bp, sp, d = x_prompt.shape
    bs, ss, _ = x_sample.shape
    past = cache_mla_latent.shape[2]
    depth = w_in.shape[0]
    wall = _prep_weights(pre_norm_w, w_in, conv_a_w, conv_a_b, lru_w_r, lru_b_r, lru_w_i, lru_b_i, lru_lambda, conv_b_w,
                         conv_c_w, conv_c_b, ssd_dt_bias, ssd_a_log, ssd_d, ssd_norm_w, mla_q_norm_w, mla_w_q_up,
                         mla_kv_norm_w, mla_w_kv_up, w_branch_out, w_out, post_norm_w)
    tabs_p = _rope_tables(jnp.arange(sp), bp)
    tabs_s = _rope_tables(past + jnp.arange(ss), bs)
    zeros_p = (jnp.zeros((bp, 3, BRANCH_W), F32), jnp.zeros((bp, BRANCH_W), F32), jnp.zeros((bp, 2, BRANCH_W), F32),
               jnp.zeros((bp, 3, BRANCH_W + 2 * N_GROUPS_C * D_STATE), F32),
               jnp.zeros((bp, N_HEADS_C, HEAD_DIM_C, D_STATE), F32))

    xp = x_prompt.reshape(bp * sp, d)
    xs = x_sample.reshape(bs * ss, d)
    hp = _rms_cast(xp, wall["pre_norm_w"][0])
    hs = _rms_cast(xs, wall["pre_norm_w"][0])
    p_st = [[] for _ in range(7)]
    s_st = [[] for _ in range(7)]
    for l in range(depth):
        w = {k: v[l] for k, v in wall.items()}
        nxt = wall["pre_norm_w"][l + 1] if l + 1 < depth else None
        xp, hp, new_p = _layer(xp, hp, bp, sp, tabs_p, 0, zeros_p, None, w, nxt)
        st_s = (state_lru_conv[l], state_lru_h[l], state_sconv[l], state_ssd_conv[l], state_ssd[l])
        xs, hs, new_s = _layer(xs, hs, bs, ss, tabs_s, past, st_s, (cache_mla_latent[l], cache_mla_kpe[l]), w, nxt)
        for i in range(7):
            p_st[i].append(new_p[i])
            s_st[i].append(new_s[i])
    p_out = [jnp.stack(v, axis=0) for v in p_st]
    s_out = [jnp.stack(v, axis=0) for v in s_st]
    return (xp.reshape(bp, sp, d), xs.reshape(bs, ss, d), *p_out, *s_out)
```

```python
import functools

import numpy as np
import jax
import jax.numpy as jnp
from jax import lax
from jax.experimental import pallas as pl
from jax.experimental.pallas import tpu as pltpu

F32 = jnp.float32
BF16 = jnp.bfloat16

D_MODEL = 4096
CHUNK = 64
BRANCH_W = 1024
N_BRANCH = 4
EPS = 1e-6
LRU_BLOCKS = 8
LRU_BLOCK = 128
LRU_C = 8.0
N_HEADS_C = 16
HEAD_DIM_C = 64
N_GROUPS_C = 2
D_STATE = 128
N_HEADS_D = 8
NOPE_DIM = 128
ROPE_DIM = 64
V_DIM = 128
Q_LORA = 1024
KV_LORA = 512
ROPE_THETA = 10000.0
ATTN_SCALE = (NOPE_DIM + ROPE_DIM) ** -0.5
HALF = ROPE_DIM // 2
BC_W = 2 * N_GROUPS_C * D_STATE

LANE = 128
SUBLANE = 8
VMEM_LIMIT = 56 * 1024 * 1024

O_BC = 7168
O_DT = 7680
O_ZC = 7696
O_KPE = 10256
O_GD = 10320
W1_W = O_DT
W2_W = O_KPE - O_ZC
C_XA, C_GA, C_BB, C_CB, C_XB, C_GB, C_XC = (i * BRANCH_W for i in range(7))
C_BCC = O_BC
C2_ZC, C2_QL, C2_KVL = 0, BRANCH_W, 2 * BRANCH_W
Q_SLAB = 256

NT_DIMS = (((1,), (1,)), ((), ()))
HI = lax.Precision.HIGHEST
EXP2_SCALE = ATTN_SCALE * 1.4426950408889634
NEG_BIG = -1e30


def _pick(n, pref):
    if n <= pref:
        return n
    t = pref
    while n % t:
        t -= SUBLANE
    return t


def _params(*sem):
    return pltpu.CompilerParams(dimension_semantics=sem, vmem_limit_bytes=VMEM_LIMIT)


def _lw(shape, l):
    idx = (l,) + (0,) * len(shape)
    return pl.BlockSpec((None,) + tuple(shape), lambda *_: idx)


def _sigmoid(x):
    return 1.0 / (1.0 + jnp.exp(-x))


def _silu(x):
    return x * _sigmoid(x)


def _softplus(x):
    return jnp.maximum(x, 0.0) + jnp.log1p(jnp.exp(-jnp.abs(x)))


def _neg_expm1(y):
    e = jnp.exp(y)
    near = jnp.where(e == 1.0, -y, (1.0 - e) * y / jnp.log(e))
    return jnp.where(y > -0.5, near, 1.0 - e)


def _rms(x, w):
    return x * lax.rsqrt(jnp.mean(x * x, axis=-1, keepdims=True) + EPS) * w


def _bdot(a, b):
    return jnp.dot(a, b, preferred_element_type=F32)


def _cast_kernel(a_ref, o_ref):
    o_ref[...] = a_ref[...].astype(o_ref.dtype)


def _shift_cast_kernel(a_ref, b_ref, o_ref, *, shift, tcw):
    x = jnp.concatenate([a_ref[...], b_ref[...]], axis=1)
    o_ref[...] = pltpu.roll(x, tcw + LANE - shift, axis=1)[:, :tcw].astype(o_ref.dtype)


def _shift_cast(w, base, shift, width, tcw, name):
    d, k, _ = w.shape
    tr = _pick(k, 512)
    a_spec = pl.BlockSpec((None, tr, tcw), lambda l, i, j: (l, i, base // tcw + j))
    o_spec = pl.BlockSpec((None, tr, tcw), lambda l, i, j: (l, i, j))
    common = dict(grid=(d, k // tr, width // tcw), out_specs=o_spec, out_shape=jax.ShapeDtypeStruct((d, k, width), BF16),
                  compiler_params=_params("parallel", "parallel", "parallel"), name=name)
    if shift == 0:
        return pl.pallas_call(_cast_kernel, in_specs=[a_spec], **common)(w)
    b_spec = pl.BlockSpec((None, tr, LANE), lambda l, i, j: (l, i, (base + (j + 1) * tcw) // LANE))
    return pl.pallas_call(functools.partial(_shift_cast_kernel, shift=shift, tcw=tcw), in_specs=[a_spec, b_spec], **common)(w, w)


def _rms_cast_kernel(x_ref, w_ref, o_ref):
    o_ref[...] = _rms(x_ref[...], w_ref[...]).astype(o_ref.dtype)


def _rms_cast(x, w, l):
    m, d = x.shape
    tm = _pick(m, 256)
    return pl.pallas_call(
        _rms_cast_kernel,
        grid=(m // tm,),
        in_specs=[pl.BlockSpec((tm, d), lambda i: (i, 0)), _lw((1, d), l)],
        out_specs=pl.BlockSpec((tm, d), lambda i: (i, 0)),
        out_shape=jax.ShapeDtypeStruct((m, d), BF16),
        compiler_params=_params("parallel"),
        name="rms_cast",
    )(x, w)


def _mm_kernel(a_ref, w_ref, o_ref):
    o_ref[...] = _bdot(a_ref[...], w_ref[...]).astype(o_ref.dtype)


def _matmul(a, w, l, ncols, tn, out_dtype, name):
    m, k = a.shape
    tm = _pick(m, 1024)
    return pl.pallas_call(
        _mm_kernel,
        grid=(m // tm, ncols // tn),
        in_specs=[pl.BlockSpec((tm, k), lambda i, j: (i, 0)), pl.BlockSpec((None, k, tn), lambda i, j: (l, 0, j))],
        out_specs=pl.BlockSpec((tm, tn), lambda i, j: (i, j)),
        out_shape=jax.ShapeDtypeStruct((m, ncols), out_dtype),
        compiler_params=_params("parallel", "arbitrary"),
        name=name,
    )(a, w)


def _merge_kernel(h_ref, ya_ref, yb_ref, yc_ref, yd_ref, g0, g1, g2, g3, b0, b1, b2, b3, o_ref):
    h = h_ref[...]
    acc = None
    for y_ref, g_ref, b_ref in ((ya_ref, g0, b0), (yb_ref, g1, b1), (yc_ref, g2, b2), (yd_ref, g3, b3)):
        term = _sigmoid(_bdot(h, g_ref[...])) * _bdot(y_ref[...], b_ref[...])
        acc = term if acc is None else acc + term
    o_ref[...] = acc.astype(o_ref.dtype)


def _merge(h, ys, w3, wb, l):
    m = h.shape[0]
    tm = _pick(m, 512)
    tn = 256
    y_spec = pl.BlockSpec((tm, BRANCH_W), lambda i, j: (i, 0))
    gate0 = BRANCH_W // tn
    g_specs = [pl.BlockSpec((None, D_MODEL, tn), functools.partial(lambda i, j, n: (l, 0, gate0 + n * (D_MODEL // tn) + j), n=n))
               for n in range(N_BRANCH)]
    b_specs = [pl.BlockSpec((None, None, BRANCH_W, tn), functools.partial(lambda i, j, n: (l, n, 0, j), n=n)) for n in range(N_BRANCH)]
    return pl.pallas_call(
        _merge_kernel,
        grid=(m // tm, D_MODEL // tn),
        in_specs=[pl.BlockSpec((tm, D_MODEL), lambda i, j: (i, 0))] + [y_spec] * 4 + g_specs + b_specs,
        out_specs=pl.BlockSpec((tm, tn), lambda i, j: (i, j)),
        out_shape=jax.ShapeDtypeStruct((m, D_MODEL), BF16),
        compiler_params=_params("parallel", "arbitrary"),
        name="gate_merge",
    )(h, *ys, w3, w3, w3, w3, wb, wb, wb, wb)


def _post_kernel(x_ref, o_ref, pw_ref, nw_ref, xn_ref, hn_ref):
    xn = x_ref[...] + _rms(o_ref[...], pw_ref[...])
    xn_ref[...] = xn
    hn_ref[...] = _rms(xn, nw_ref[...]).astype(hn_ref.dtype)


def _post_last_kernel(x_ref, o_ref, pw_ref, xn_ref):
    xn_ref[...] = x_ref[...] + _rms(o_ref[...], pw_ref[...])


def _post(x, out, post_w, pre_w, l, last):
    m, d = x.shape
    tm = _pick(m, 256)
    row = pl.BlockSpec((tm, d), lambda i: (i, 0))
    if last:
        return pl.pallas_call(
            _post_last_kernel, grid=(m // tm,), in_specs=[row, row, _lw((1, d), l)], out_specs=row,
            out_shape=jax.ShapeDtypeStruct((m, d), F32), compiler_params=_params("parallel"), name="post_last",
        )(x, out, post_w), None
    return pl.pallas_call(
        _post_kernel, grid=(m // tm,), in_specs=[row, row, _lw((1, d), l), _lw((1, d), l + 1)], out_specs=[row, row],
        out_shape=[jax.ShapeDtypeStruct((m, d), F32), jax.ShapeDtypeStruct((m, d), BF16)],
        compiler_params=_params("parallel"), name="post",
    )(x, out, post_w, pre_w)


def _conv_tile(ext_s, x, w_ref, width, tl):
    ext_s[8:8 + tl, :] = x
    k0 = 9 - width
    y = ext_s[k0:k0 + tl, :] * w_ref[0:1, :]
    for j in range(1, width - 1):
        y = y + ext_s[k0 + j:k0 + j + tl, :] * w_ref[j:j + 1, :]
    return y + x * w_ref[width - 1:width, :]


def _conv_carry(ext_s, width, tl):
    tail = ext_s[tl + 9 - width:tl + 8, :]
    ext_s[9 - width:8, :] = tail
    return tail


def _lru_kernel(xa_ref, ga_ref, cprev_ref, h0_ref, cw_ref, cb_ref, wr_ref, br_ref, wi_ref, bi_ref, lam_ref,
                y_ref, hlast_ref, cnew_ref, ext_s, a_s, u_s, h_s, *, tl, nl):
    l = pl.program_id(1)

    @pl.when(l == 0)
    def _():
        ext_s[5:8, :] = cprev_ref[...]
        h_s[...] = h0_ref[...]

    xc = _conv_tile(ext_s, xa_ref[...], cw_ref, 4, tl) + cb_ref[...]
    tail = _conv_carry(ext_s, 4, tl)

    xcb = xc.astype(BF16)
    rs, gs = [], []
    for n in range(LRU_BLOCKS):
        blk = xcb[:, n * LRU_BLOCK:(n + 1) * LRU_BLOCK]
        rs.append(_bdot(blk, wr_ref[n]))
        gs.append(_bdot(blk, wi_ref[n]))
    r = _sigmoid(jnp.concatenate(rs, axis=1) + br_ref[...])
    gate_i = _sigmoid(jnp.concatenate(gs, axis=1) + bi_ref[...])
    log_a = (-LRU_C * r) * _softplus(-lam_ref[...])
    a_s[...] = jnp.exp(log_a)
    u_s[...] = jnp.sqrt(_neg_expm1(2.0 * log_a)) * (gate_i * xc)

    def step(t, h):
        h = a_s[pl.ds(t, 1), :] * h + u_s[pl.ds(t, 1), :]
        u_s[pl.ds(t, 1), :] = h
        return h

    h = lax.fori_loop(0, tl, step, h_s[...], unroll=8)
    h_s[...] = h
    y_ref[...] = (u_s[...] * _silu(ga_ref[...])).astype(y_ref.dtype)

    @pl.when(l == nl - 1)
    def _():
        hlast_ref[...] = h
        cnew_ref[...] = tail


def _lru(p1, cprev, h0, w, l):
    b, seq, _ = p1.shape
    tl = _pick(seq, 256)
    nl = seq // tl
    col = lambda c: pl.BlockSpec((None, tl, BRANCH_W), lambda i, t: (i, t, c // BRANCH_W))
    st3 = pl.BlockSpec((None, 3, BRANCH_W), lambda i, t: (i, 0, 0))
    st1 = pl.BlockSpec((None, 1, BRANCH_W), lambda i, t: (i, 0, 0))
    vec = _lw((1, BRANCH_W), l)
    blk = _lw((LRU_BLOCKS, LRU_BLOCK, LRU_BLOCK), l)
    return pl.pallas_call(
        functools.partial(_lru_kernel, tl=tl, nl=nl),
        grid=(b, nl),
        in_specs=[col(C_XA), col(C_GA), st3, st1, _lw((4, BRANCH_W), l), vec, blk, vec, blk, vec, vec],
        out_specs=[pl.BlockSpec((None, tl, BRANCH_W), lambda i, t: (i, t, 0)), st1, st3],
        out_shape=[jax.ShapeDtypeStruct((b, seq, BRANCH_W), BF16), jax.ShapeDtypeStruct((b, 1, BRANCH_W), F32),
                   jax.ShapeDtypeStruct((b, 3, BRANCH_W), F32)],
        scratch_shapes=[pltpu.VMEM((tl + 8, BRANCH_W), F32), pltpu.VMEM((tl, BRANCH_W), F32),
                        pltpu.VMEM((tl, BRANCH_W), F32), pltpu.VMEM((1, BRANCH_W), F32)],
        compiler_params=_params("parallel", "arbitrary"),
        name="rglru",
    )(p1, p1, cprev, h0, w["conv_a_w"], w["conv_a_b"], w["lru_w_r"], w["lru_b_r"], w["lru_w_i"], w["lru_b_i"], w["lru_lambda"])


def _sconv_kernel(bb_ref, cb_ref, xb_ref, gb_ref, cprev_ref, cw_ref, y_ref, cnew_ref, ext_s, *, tl, nl):
    l = pl.program_id(1)

    @pl.when(l == 0)
    def _():
        ext_s[6:8, :] = cprev_ref[...]

    v = _conv_tile(ext_s, cb_ref[...] * xb_ref[...], cw_ref, 3, tl)
    tail = _conv_carry(ext_s, 3, tl)
    y_ref[...] = (bb_ref[...] * v * _silu(gb_ref[...])).astype(y_ref.dtype)

    @pl.when(l == nl - 1)
    def _():
        cnew_ref[...] = tail


def _sconv(p1, cprev, w, l):
    b, seq, _ = p1.shape
    tl = _pick(seq, 256)
    nl = seq // tl
    col = lambda c: pl.BlockSpec((None, tl, BRANCH_W), lambda i, t: (i, t, c // BRANCH_W))
    st2 = pl.BlockSpec((None, 2, BRANCH_W), lambda i, t: (i, 0, 0))
    return pl.pallas_call(
        functools.partial(_sconv_kernel, tl=tl, nl=nl),
        grid=(b, nl),
        in_specs=[col(C_BB), col(C_CB), col(C_XB), col(C_GB), st2, _lw((3, BRANCH_W), l)],
        out_specs=[pl.BlockSpec((None, tl, BRANCH_W), lambda i, t: (i, t, 0)), st2],
        out_shape=[jax.ShapeDtypeStruct((b, seq, BRANCH_W), BF16), jax.ShapeDtypeStruct((b, 2, BRANCH_W), F32)],
        scratch_shapes=[pltpu.VMEM((tl + 8, BRANCH_W), F32)],
        compiler_params=_params("parallel", "arbitrary"),
        name="sconv",
    )(p1, p1, p1, p1, cprev, w["conv_b_w"])


def _ssd_kernel(xc_ref, bcc_ref, dt_ref, zc_ref, cpx_ref, cpb_ref, st0_ref, cwx_ref, cbx_ref, cwb_ref, cbb_ref,
                dtb_ref, alog_ref, dexp_ref, nw_ref, e_ref,
                y_ref, cnx_ref, cnb_ref, st_ref, extx_s, extb_s, st_s, *, q, nl):
    l = pl.program_id(1)

    @pl.when(l == 0)
    def _():
        extx_s[5:8, :] = cpx_ref[...]
        extb_s[5:8, :] = cpb_ref[...]
        st_s[...] = st0_ref[...]

    xs = _silu(_conv_tile(extx_s, xc_ref[...], cwx_ref, 4, q) + cbx_ref[...])
    tailx = _conv_carry(extx_s, 4, q)
    bcs = _silu(_conv_tile(extb_s, bcc_ref[...], cwb_ref, 4, q) + cbb_ref[...])
    tailb = _conv_carry(extb_s, 4, q)

    dt = _softplus(dt_ref[...] + dtb_ref[...])
    d_a = dt * (-jnp.exp(alog_ref[...]))
    rows = lax.broadcasted_iota(jnp.int32, (q, q), 0)
    cols = lax.broadcasted_iota(jnp.int32, (q, q), 1)
    tri = rows >= cols
    acum = jnp.dot(tri.astype(F32), d_a, precision=HI, preferred_element_type=F32)
    eye = (lax.broadcasted_iota(jnp.int32, (LANE, LANE), 0) == lax.broadcasted_iota(jnp.int32, (LANE, LANE), 1))
    acum_t = lax.dot_general(eye.astype(F32), acum, NT_DIMS, precision=HI, preferred_element_type=F32)
    expand = e_ref[...]
    dt_x = jnp.dot(dt, expand, precision=HI, preferred_element_type=F32)
    acum_x = jnp.dot(acum, expand, precision=HI, preferred_element_type=F32)
    end_x = acum_x[q - 1:q, :]
    xdt = xs * dt_x
    xw = (xdt * jnp.exp(end_x - acum_x)).astype(BF16)
    xdt_b = xdt.astype(BF16)
    eacum_x = jnp.exp(acum_x)
    cdecay_x = jnp.exp(end_x)
    eye_b = eye.astype(BF16)

    y_parts = []
    for g in range(N_GROUPS_C):
        bg = bcs[:, g * D_STATE:(g + 1) * D_STATE].astype(BF16)
        cg = bcs[:, (N_GROUPS_C + g) * D_STATE:(N_GROUPS_C + g + 1) * D_STATE].astype(BF16)
        cb = lax.dot_general(cg, bg, NT_DIMS, preferred_element_type=F32)
        bg_t = lax.dot_general(eye_b, bg, NT_DIMS, preferred_element_type=F32).astype(BF16)
        for e in range(N_HEADS_C // N_GROUPS_C):
            h = g * (N_HEADS_C // N_GROUPS_C) + e
            lo = h * HEAD_DIM_C
            seg = acum_x[:, lo:lo + q] - acum_t[h:h + 1, :]
            decay = jnp.exp(jnp.where(tri, seg, -jnp.inf))
            y_diag = _bdot((cb * decay).astype(BF16), xdt_b[:, lo:lo + HEAD_DIM_C])
            prev = st_s[h]
            y_off = _bdot(cg, prev.astype(BF16)) * eacum_x[:, lo:lo + HEAD_DIM_C]
            st_s[h] = prev * cdecay_x[:, lo:lo + HEAD_DIM_C] + _bdot(bg_t, xw[:, lo:lo + HEAD_DIM_C])
            y_parts.append(y_diag + y_off)
    y = jnp.concatenate(y_parts, axis=1) + dexp_ref[...] * xs

    gated = y * _silu(zc_ref[...])
    half = BRANCH_W // N_GROUPS_C
    normed = jnp.concatenate(
        [gated[:, i * half:(i + 1) * half]
         * lax.rsqrt(jnp.mean(gated[:, i * half:(i + 1) * half] ** 2, axis=-1, keepdims=True) + EPS)
         for i in range(N_GROUPS_C)], axis=1)
    y_ref[...] = (normed * nw_ref[...]).astype(y_ref.dtype)

    @pl.when(l == nl - 1)
    def _():
        cnx_ref[...] = tailx
        cnb_ref[...] = tailb
        st_ref[...] = st_s[...]


def _ssd(p1, p2, pkd, cprev_x, cprev_b, st0, w, l):
    b, seq, _ = p1.shape
    q = CHUNK if seq % CHUNK == 0 else seq
    nl = seq // q
    st_shape = (N_HEADS_C, D_STATE, HEAD_DIM_C)
    colx = lambda c, wd: pl.BlockSpec((None, q, wd), lambda i, t: (i, t, c // wd))
    st3 = lambda wd: pl.BlockSpec((None, 3, wd), lambda i, t: (i, 0, 0))
    st_spec = pl.BlockSpec((None,) + st_shape, lambda i, t: (i, 0, 0, 0))
    return pl.pallas_call(
        functools.partial(_ssd_kernel, q=q, nl=nl),
        grid=(b, nl),
        in_specs=[colx(C_XC, BRANCH_W), colx(C_BCC, BC_W), colx(LANE, LANE), colx(C2_ZC, BRANCH_W),
                  st3(BRANCH_W), st3(BC_W), st_spec,
                  _lw((4, BRANCH_W), l), _lw((1, BRANCH_W), l), _lw((4, BC_W), l), _lw((1, BC_W), l),
                  _lw((1, LANE), l), _lw((1, LANE), l), _lw((1, BRANCH_W), l), _lw((1, BRANCH_W), l),
                  pl.BlockSpec((LANE, BRANCH_W), lambda i, t: (0, 0))],
        out_specs=[pl.BlockSpec((None, q, BRANCH_W), lambda i, t: (i, t, 0)), st3(BRANCH_W), st3(BC_W), st_spec],
        out_shape=[jax.ShapeDtypeStruct((b, seq, BRANCH_W), BF16), jax.ShapeDtypeStruct((b, 3, BRANCH_W), F32),
                   jax.ShapeDtypeStruct((b, 3, BC_W), F32), jax.ShapeDtypeStruct((b,) + st_shape, F32)],
        scratch_shapes=[pltpu.VMEM((q + 8, BRANCH_W), F32), pltpu.VMEM((q + 8, BC_W), F32), pltpu.VMEM(st_shape, F32)],
        compiler_params=_params("parallel", "arbitrary"),
        name="ssd",
    )(p1, p1, pkd, p2, cprev_x, cprev_b, st0, w["conv_cx_w"], w["conv_cx_b"], w["conv_cb_w"], w["conv_cb_b"],
      w["dt_bias"], w["a_log"], w["d_exp"], w["ssd_norm_w"], w["expand"])


def _mla_prep_kernel(ql_ref, kvl_ref, kraw_ref, cq_ref, sq_ref, ck_ref, sk_ref, qnw_ref, kvnw_ref, wq_ref, wqs_ref,
                     q_ref, ckv_ref, kpe_ref, kpe128_ref):
    qn = _rms(ql_ref[...], qnw_ref[...]).astype(BF16)
    q = _bdot(qn, wq_ref[...])
    q_sw = _bdot(qn, wqs_ref[...])
    cos_q, sin_q = cq_ref[...], sq_ref[...]
    for h in range(N_HEADS_D):
        sl = slice(h * Q_SLAB, (h + 1) * Q_SLAB)
        q_ref[:, sl] = (q[:, sl] * cos_q + q_sw[:, sl] * sin_q).astype(q_ref.dtype)
    ckv_ref[...] = _rms(kvl_ref[...], kvnw_ref[...])
    kraw = kraw_ref[...]
    lane = lax.broadcasted_iota(jnp.int32, kraw.shape, 1)
    swapped = jnp.where(lane < HALF, pltpu.roll(kraw, LANE - HALF, axis=1), pltpu.roll(kraw, HALF, axis=1))
    kr = kraw * ck_ref[...] + swapped * sk_ref[...]
    kpe128_ref[...] = kr
    kpe_ref[...] = kr[:, :ROPE_DIM]


def _mla_prep(p2, pkd, tabs, w, l):
    m = p2.shape[0]
    tm = _pick(m, 256)
    col = lambda c, wd: pl.BlockSpec((tm, wd), lambda i: (i, c // wd))
    row = lambda wd: pl.BlockSpec((tm, wd), lambda i: (i, 0))
    qw = N_HEADS_D * Q_SLAB
    return pl.pallas_call(
        _mla_prep_kernel,
        grid=(m // tm,),
        in_specs=[col(C2_QL, Q_LORA), col(C2_KVL, KV_LORA), col(0, LANE),
                  row(Q_SLAB), row(Q_SLAB), row(LANE), row(LANE),
                  _lw((1, Q_LORA), l), _lw((1, KV_LORA), l), _lw((Q_LORA, qw), l), _lw((Q_LORA, qw), l)],
        out_specs=[row(qw), row(KV_LORA), row(ROPE_DIM), row(LANE)],
        out_shape=[jax.ShapeDtypeStruct((m, qw), BF16), jax.ShapeDtypeStruct((m, KV_LORA), F32),
                   jax.ShapeDtypeStruct((m, ROPE_DIM), F32), jax.ShapeDtypeStruct((m, LANE), F32)],
        compiler_params=_params("parallel"),
        name="mla_prep",
    )(p2, p2, pkd, tabs["cos_q"], tabs["sin_q"], tabs["cos_k"], tabs["sin_k"],
      w["mla_q_norm_w"], w["mla_kv_norm_w"], w["wq"], w["wq_sw"])


def _kv_expand_kernel(ckv_ref, kpe_ref, wk_ref, wv_ref, k_ref, v_ref):
    c = ckv_ref[...].astype(BF16)
    kn = _bdot(c, wk_ref[...])
    v_ref[...] = _bdot(c, wv_ref[...]).astype(v_ref.dtype)
    kr = kpe_ref[...]
    for h in range(N_HEADS_D):
        k_ref[:, h * Q_SLAB:h * Q_SLAB + NOPE_DIM] = kn[:, h * NOPE_DIM:(h + 1) * NOPE_DIM].astype(k_ref.dtype)
        k_ref[:, h * Q_SLAB + NOPE_DIM:(h + 1) * Q_SLAB] = kr.astype(k_ref.dtype)


def _kv_expand(ckv, kpe128, w, l):
    r = ckv.shape[0]
    tr = _pick(r, 512)
    row = lambda wd: pl.BlockSpec((tr, wd), lambda i: (i, 0))
    kw, vw = N_HEADS_D * Q_SLAB, N_HEADS_D * V_DIM
    return pl.pallas_call(
        _kv_expand_kernel,
        grid=(r // tr,),
        in_specs=[row(KV_LORA), row(LANE), _lw((KV_LORA, N_HEADS_D * NOPE_DIM), l), _lw((KV_LORA, vw), l)],
        out_specs=[row(kw), row(vw)],
        out_shape=[jax.ShapeDtypeStruct((r, kw), BF16), jax.ShapeDtypeStruct((r, vw), BF16)],
        compiler_params=_params("parallel"),
        name="kv_expand",
    )(ckv, kpe128, w["wk"], w["wv"])


def _attn_kernel(qi_ref, kj_ref, fin_ref, q_ref, k_ref, v_ref, gd_ref, o_ref, m_s, l_s, acc_s, *, tq, tk, q_off):
    step = pl.program_id(1)
    j = kj_ref[step]
    q_lo = q_off + qi_ref[step] * tq
    k_lo = j * tk
    first_bound = (q_lo // CHUNK + 1) * CHUNK

    @pl.when(j == 0)
    def _():
        m_s[...] = jnp.full(m_s.shape, NEG_BIG, F32)
        l_s[...] = jnp.zeros(l_s.shape, F32)
        acc_s[...] = jnp.zeros(acc_s.shape, F32)

    def block(masked):
        if masked:
            qpos = q_lo + lax.broadcasted_iota(jnp.int32, (tq, tk), 0)
            kpos = k_lo + lax.broadcasted_iota(jnp.int32, (tq, tk), 1)
            visible = kpos < (qpos // CHUNK + 1) * CHUNK
        for h in range(N_HEADS_D):
            q = q_ref[:, h * Q_SLAB:(h + 1) * Q_SLAB]
            k = k_ref[:, h * Q_SLAB:(h + 1) * Q_SLAB]
            s = lax.dot_general(q, k, NT_DIMS, preferred_element_type=F32)
            if masked:
                s = jnp.where(visible, s, NEG_BIG)
            m_prev = m_s[h]
            m_next = jnp.maximum(m_prev, jnp.max(s, axis=1, keepdims=True))
            alpha = jnp.exp2((m_prev - m_next) * EXP2_SCALE)
            pr = jnp.exp2((s - m_next[:, 0:1]) * EXP2_SCALE)
            l_s[h] = alpha * l_s[h] + jnp.sum(pr, axis=1, keepdims=True)
            m_s[h] = m_next
            acc_s[h] = acc_s[h] * alpha + _bdot(pr.astype(BF16), v_ref[:, h * V_DIM:(h + 1) * V_DIM])

    @pl.when(k_lo + tk <= first_bound)
    def _():
        block(False)

    @pl.when(k_lo + tk > first_bound)
    def _():
        block(True)

    @pl.when(fin_ref[step] == 1)
    def _():
        gate = _silu(gd_ref[...])
        for h in range(N_HEADS_D):
            sl = slice(h * V_DIM, (h + 1) * V_DIM)
            o_ref[:, sl] = (acc_s[h] / l_s[h] * gate[:, sl]).astype(o_ref.dtype)


def _attention(q, k, v, gd, q_off):
    b, lq, _ = q.shape
    lk = k.shape[1]
    tq = _pick(lq, 512)
    tk = _pick(lk, 1024) if lk % 1024 == 0 else lk
    nq, nk = lq // tq, lk // tk
    qi, kj, fin = [], [], []
    for i in range(nq):
        last_bound = ((q_off + (i + 1) * tq - 1) // CHUNK + 1) * CHUNK
        last_j = min(nk - 1, (last_bound - 1) // tk)
        for j in range(last_j + 1):
            qi.append(i)
            kj.append(j)
            fin.append(int(j == last_j))
    steps = len(qi)
    to_arr = lambda a: jnp.asarray(np.asarray(a, np.int32))
    qw, vw = N_HEADS_D * Q_SLAB, N_HEADS_D * V_DIM
    grid_spec = pltpu.PrefetchScalarGridSpec(
        num_scalar_prefetch=3,
        grid=(b, steps),
        in_specs=[pl.BlockSpec((None, tq, qw), lambda bi, s, qi_r, kj_r, fin_r: (bi, qi_r[s], 0)),
                  pl.BlockSpec((None, tk, qw), lambda bi, s, qi_r, kj_r, fin_r: (bi, kj_r[s], 0)),
                  pl.BlockSpec((None, tk, vw), lambda bi, s, qi_r, kj_r, fin_r: (bi, kj_r[s], 0)),
                  pl.BlockSpec((None, tq, vw), lambda bi, s, qi_r, kj_r, fin_r: (bi, qi_r[s], 0))],
        out_specs=pl.BlockSpec((None, tq, vw), lambda bi, s, qi_r, kj_r, fin_r: (bi, qi_r[s], 0)),
        scratch_shapes=[pltpu.VMEM((N_HEADS_D, tq, LANE), F32), pltpu.VMEM((N_HEADS_D, tq, LANE), F32),
                        pltpu.VMEM((N_HEADS_D, tq, V_DIM), F32)],
    )
    return pl.pallas_call(
        functools.partial(_attn_kernel, tq=tq, tk=tk, q_off=q_off),
        grid_spec=grid_spec,
        out_shape=jax.ShapeDtypeStruct((b, lq, vw), BF16),
        compiler_params=_params("parallel", "arbitrary"),
        name="mla_attention",
    )(to_arr(qi), to_arr(kj), to_arr(fin), q, k, v, gd)


def _prep_weights(pre_norm_w, w_in, conv_a_w, conv_a_b, lru_w_r, lru_b_r, lru_w_i, lru_b_i, lru_lambda, conv_b_w,
                  conv_c_w, conv_c_b, ssd_dt_bias, ssd_a_log, ssd_d, ssd_norm_w, mla_q_norm_w, mla_w_q_up,
                  mla_kv_norm_w, mla_w_kv_up, w_branch_out, w_out, post_norm_w):
    d = w_in.shape[0]
    w1 = _shift_cast(w_in, 0, 0, W1_W, 1280, "pack_w1")
    w2 = _shift_cast(w_in, O_DT, O_ZC - O_DT, W2_W, 512, "pack_w2")
    w3 = _shift_cast(w_in, O_GD - 80, 80, w_in.shape[2] - O_GD, 1024, "pack_w3")
    w_kraw = _shift_cast(w_in, O_KPE - 16, 16, LANE, LANE, "pack_wk")
    w_dt = _shift_cast(w_in, O_DT, 0, LANE, LANE, "pack_wdt")
    wkd = jnp.concatenate([w_kraw, w_dt], axis=-1)

    wq = mla_w_q_up.reshape(d, Q_LORA, N_HEADS_D, NOPE_DIM + ROPE_DIM)
    nope, x1, x2 = wq[..., :NOPE_DIM], wq[..., NOPE_DIM:NOPE_DIM + HALF], wq[..., NOPE_DIM + HALF:]
    zq = lambda n: jnp.zeros((d, Q_LORA, N_HEADS_D, n), wq.dtype)
    wq_main = jnp.concatenate([nope, x1, x2, zq(Q_SLAB - NOPE_DIM - ROPE_DIM)], -1).reshape(d, Q_LORA, -1).astype(BF16)
    wq_sw = jnp.concatenate([zq(NOPE_DIM), -x2, x1, zq(Q_SLAB - NOPE_DIM - ROPE_DIM)], -1).reshape(d, Q_LORA, -1).astype(BF16)
    wkv = mla_w_kv_up.reshape(d, KV_LORA, N_HEADS_D, NOPE_DIM + V_DIM)
    wk = wkv[..., :NOPE_DIM].reshape(d, KV_LORA, -1).astype(BF16)
    wv = wkv[..., NOPE_DIM:].reshape(d, KV_LORA, -1).astype(BF16)

    pad_heads = lambda a: jnp.pad(a, ((0, 0), (0, LANE - N_HEADS_C)))[:, None, :]
    expand = (jnp.arange(LANE)[:, None] == (jnp.arange(BRANCH_W) // HEAD_DIM_C)[None, :]).astype(F32)
    vec = lambda a: a[:, None, :]
    return dict(
        pre_norm_w=vec(pre_norm_w), post_norm_w=vec(post_norm_w), w1=w1, w2=w2, w3=w3, wkd=wkd,
        w_branch=w_branch_out.astype(BF16), w_out=w_out.astype(BF16),
        conv_a_w=conv_a_w, conv_a_b=vec(conv_a_b), lru_w_r=lru_w_r.astype(BF16), lru_b_r=vec(lru_b_r),
        lru_w_i=lru_w_i.astype(BF16), lru_b_i=vec(lru_b_i), lru_lambda=vec(lru_lambda), conv_b_w=conv_b_w,
        conv_cx_w=conv_c_w[..., :BRANCH_W], conv_cx_b=vec(conv_c_b[..., :BRANCH_W]),
        conv_cb_w=conv_c_w[..., BRANCH_W:], conv_cb_b=vec(conv_c_b[..., BRANCH_W:]),
        dt_bias=pad_heads(ssd_dt_bias), a_log=pad_heads(ssd_a_log), d_exp=vec(jnp.repeat(ssd_d, HEAD_DIM_C, axis=-1)),
        ssd_norm_w=vec(ssd_norm_w), expand=expand,
        mla_q_norm_w=vec(mla_q_norm_w), mla_kv_norm_w=vec(mla_kv_norm_w), wq=wq_main, wq_sw=wq_sw, wk=wk, wv=wv,
    )


def _rope_tables(pos, reps):
    inv = ROPE_THETA ** (-jnp.arange(HALF, dtype=F32) / HALF)
    ang = pos.astype(F32)[:, None] * inv[None, :]
    cos, sin = jnp.cos(ang), jnp.sin(ang)
    n = pos.shape[0]
    z = lambda w: jnp.zeros((n, w), F32)
    tabs = dict(
        cos_q=jnp.concatenate([jnp.ones((n, NOPE_DIM), F32), cos, cos, z(Q_SLAB - NOPE_DIM - ROPE_DIM)], 1),
        sin_q=jnp.concatenate([z(NOPE_DIM), sin, sin, z(Q_SLAB - NOPE_DIM - ROPE_DIM)], 1),
        cos_k=jnp.concatenate([cos, cos, z(LANE - ROPE_DIM)], 1),
        sin_k=jnp.concatenate([-sin, sin, z(LANE - ROPE_DIM)], 1),
    )
    return {k: jnp.tile(v, (reps, 1)) for k, v in tabs.items()}


def _layer(x, h, b, seq, tabs, q_off, state, past, w, l, last):
    conv_a_prev, h0, conv_b_prev, conv_c_prev, ssd0 = state
    m = b * seq
    p1 = _matmul(h, w["w1"], l, W1_W, 1280, F32, "in_proj1")
    p2 = _matmul(h, w["w2"], l, W2_W, 1280, F32, "in_proj2")
    gd = _matmul(h, w["w3"], l, BRANCH_W, BRANCH_W, F32, "in_proj_gd")
    pkd = _matmul(h, w["wkd"], l, 2 * LANE, 2 * LANE, F32, "in_proj_kd")
    p1_3, p2_3, pkd_3 = p1.reshape(b, seq, -1), p2.reshape(b, seq, -1), pkd.reshape(b, seq, -1)

    y_a, h_last, conv_a_new = _lru(p1_3, conv_a_prev, h0[:, None, :], w, l)
    y_b, conv_b_new = _sconv(p1_3, conv_b_prev, w, l)
    y_c, conv_cx_new, conv_cb_new, ssd_t = _ssd(p1_3, p2_3, pkd_3, conv_c_prev[..., :BRANCH_W], conv_c_prev[..., BRANCH_W:],
                                                 jnp.swapaxes(ssd0, -1, -2), w, l)
    q, ckv, kpe, kpe128 = _mla_prep(p2, pkd, tabs, w, l)
    if past is None:
        ckv_all, kpe_all, lk = ckv, kpe128, seq
    else:
        past_ckv, past_kpe = past
        lk = past_ckv.shape[1] + seq
        ckv_all = jnp.concatenate([past_ckv, ckv.reshape(b, seq, KV_LORA)], axis=1).reshape(b * lk, KV_LORA)
        past_kpe128 = jnp.pad(past_kpe, ((0, 0), (0, 0), (0, LANE - ROPE_DIM)))
        kpe_all = jnp.concatenate([past_kpe128, kpe128.reshape(b, seq, LANE)], axis=1).reshape(b * lk, LANE)
    k, v = _kv_expand(ckv_all, kpe_all, w, l)
    y_d = _attention(q.reshape(b, seq, -1), k.reshape(b, lk, -1), v.reshape(b, lk, -1), gd.reshape(b, seq, -1), q_off)

    merged = _merge(h, [y.reshape(m, BRANCH_W) for y in (y_a, y_b, y_c, y_d)], w["w3"], w["w_branch"], l)
    out = _matmul(merged, w["w_out"], l, D_MODEL, 1024, F32, "out_proj")
    x_new, h_next = _post(x, out, w["post_norm_w"], w["pre_norm_w"], l, last)
    new_state = (ckv.reshape(b, seq, KV_LORA), kpe.reshape(b, seq, ROPE_DIM), conv_a_new, h_last[:, 0, :], conv_b_new,
                 jnp.concatenate([conv_cx_new, conv_cb_new], axis=-1), jnp.swapaxes(ssd_t, -1, -2))
    return x_new, h_next, new_state


def kernel(x_prompt, x_sample, cache_mla_latent, cache_mla_kpe, state_lru_conv, state_lru_h, state_sconv, state_ssd_conv, state_ssd, pre_norm_w, w_in, conv_a_w, conv_a_b, lru_w_r, lru_b_r, lru_w_i, lru_b_i, lru_lambda, conv_b_w, conv_c_w, conv_c_b, ssd_dt_bias, ssd_a_log, ssd_d, ssd_norm_w, mla_q_norm_w, mla_w_q_up, mla_kv_norm_w, mla_w_kv_up, w_branch_out, w_out, post_norm_w):
    bp, sp, d = x_prompt.shape
    bs, ss, _ = x_sample.shape
    past = cache_mla_latent.shape[2]
    depth = w_in.shape[0]
    w = _prep_weights(pre_norm_w, w_in, conv_a_w, conv_a_b, lru_w_r, lru_b_r, lru_w_i, lru_b_i, lru_lambda, conv_b_w,
                      conv_c_w, conv_c_b, ssd_dt_bias, ssd_a_log, ssd_d, ssd_norm_w, mla_q_norm_w, mla_w_q_up,
                      mla_kv_norm_w, mla_w_kv_up, w_branch_out, w_out, post_norm_w)
    tabs_p = _rope_tables(jnp.arange(sp), bp)
    tabs_s = _rope_tables(past + jnp.arange(ss), bs)
    zeros_p = (jnp.zeros((bp, 3, BRANCH_W), F32), jnp.zeros((bp, BRANCH_W), F32), jnp.zeros((bp, 2, BRANCH_W), F32),
               jnp.zeros((bp, 3, BRANCH_W + BC_W), F32), jnp.zeros((bp, N_HEADS_C, HEAD_DIM_C, D_STATE), F32))

    xp = x_prompt.reshape(bp * sp, d)
    xs = x_sample.reshape(bs * ss, d)
    hp = _rms_cast(xp, w["pre_norm_w"], 0)
    hs = _rms_cast(xs, w["pre_norm_w"], 0)
    p_st = [[] for _ in range(7)]
    s_st = [[] for _ in range(7)]
    for l in range(depth):
        last = l == depth - 1
        xp, hp, new_p = _layer(xp, hp, bp, sp, tabs_p, 0, zeros_p, None, w, l, last)
        st_s = (state_lru_conv[l], state_lru_h[l], state_sconv[l], state_ssd_conv[l], state_ssd[l])
        xs, hs, new_s = _layer(xs, hs, bs, ss, tabs_s, past, st_s, (cache_mla_latent[l], cache_mla_kpe[l]), w, l, last)
        for i in range(7):
            p_st[i].append(new_p[i])
            s_st[i].append(new_s[i])
    p_out = [jnp.stack(v, axis=0) for v in p_st]
    s_out = [jnp.stack(v, axis=0) for v in s_st]
    return (xp.reshape(bp, sp, d), xs.reshape(bs, ss, d), *p_out, *s_out)
```

```python
import functools

import numpy as np
import jax
import jax.numpy as jnp
from jax import lax
from jax.experimental import pallas as pl
from jax.experimental.pallas import tpu as pltpu

F32 = jnp.float32
BF16 = jnp.bfloat16

D_MODEL = 4096
CHUNK = 64
BRANCH_W = 1024
N_BRANCH = 4
EPS = 1e-6
LRU_BLOCKS = 8
LRU_BLOCK = 128
LRU_C = 8.0
N_HEADS_C = 16
HEAD_DIM_C = 64
N_GROUPS_C = 2
D_STATE = 128
N_HEADS_D = 8
NOPE_DIM = 128
ROPE_DIM = 64
V_DIM = 128
Q_LORA = 1024
KV_LORA = 512
ROPE_THETA = 10000.0
ATTN_SCALE = (NOPE_DIM + ROPE_DIM) ** -0.5
HALF = ROPE_DIM // 2
BC_W = 2 * N_GROUPS_C * D_STATE

LANE = 128
SUBLANE = 8
VMEM_LIMIT = 56 * 1024 * 1024

O_BC = 7168
O_DT = 7680
O_ZC = 7696
O_QL = 8720
O_KVL = 9744
O_KPE = 10256
O_GD = 10320
O_MG = 11344
PACK_TILE = 512
PACK_SRC = (tuple(range(0, O_DT, PACK_TILE)) + (O_DT,) + (O_ZC, O_ZC + PACK_TILE) + (O_QL, O_QL + PACK_TILE)
            + (O_KVL, O_KPE) + (O_GD, O_GD + PACK_TILE))
PACK_W = len(PACK_SRC) * PACK_TILE
C_XA, C_GA, C_BB, C_CB, C_XB, C_GB, C_XC = (i * BRANCH_W for i in range(7))
C_BCC = 7168
C_DT = 7680
C_ZC = 8192
C_QL = 9216
C_KVL = 10240
C_KR = 10752
C_GD = 11264
Q_SLAB = 256

NT_DIMS = (((1,), (1,)), ((), ()))
HI = lax.Precision.HIGHEST
EXP2_SCALE = ATTN_SCALE * 1.4426950408889634
NEG_BIG = -1e30


def _pick(n, pref):
    if n <= pref:
        return n
    t = pref
    while n % t:
        t -= SUBLANE
    return t


def _params(*sem):
    return pltpu.CompilerParams(dimension_semantics=sem, vmem_limit_bytes=VMEM_LIMIT)


def _lw(shape, l):
    idx = (l,) + (0,) * len(shape)
    return pl.BlockSpec((None,) + tuple(shape), lambda *_: idx)


def _sigmoid(x):
    return 1.0 / (1.0 + jnp.exp(-x))


def _silu(x):
    return x * _sigmoid(x)


def _softplus(x):
    return jnp.maximum(x, 0.0) + jnp.log1p(jnp.exp(-jnp.abs(x)))


def _neg_expm1(y):
    e = jnp.exp(y)
    near = jnp.where(e == 1.0, -y, (1.0 - e) * y / jnp.log(e))
    return jnp.where(y > -0.5, near, 1.0 - e)


def _rms(x, w):
    return x * lax.rsqrt(jnp.mean(x * x, axis=-1, keepdims=True) + EPS) * w


def _bdot(a, b):
    return jnp.dot(a, b, preferred_element_type=F32)


def _pack_kernel(off_ref, a_ref, o_ref):
    del off_ref
    o_ref[...] = a_ref[0].T.astype(o_ref.dtype)


def _pack_w_in(w_in):
    d, k, n = w_in.shape
    wt = jnp.swapaxes(w_in, 1, 2)
    src = PACK_SRC + tuple(range(O_MG, n, PACK_TILE))
    grid_spec = pltpu.PrefetchScalarGridSpec(
        num_scalar_prefetch=1,
        grid=(d, len(src)),
        in_specs=[pl.BlockSpec((pl.Element(1), pl.Element(PACK_TILE), pl.Element(k)), lambda l, j, off: (l, pl.multiple_of(off[j], 16), 0))],
        out_specs=pl.BlockSpec((None, k, PACK_TILE), lambda l, j, off: (l, 0, j)),
    )
    return pl.pallas_call(
        _pack_kernel, grid_spec=grid_spec, out_shape=jax.ShapeDtypeStruct((d, k, len(src) * PACK_TILE), BF16),
        compiler_params=_params("parallel", "parallel"), name="pack_w_in",
    )(jnp.asarray(np.asarray(src, np.int32)), wt)


def _rms_cast_kernel(x_ref, w_ref, o_ref):
    o_ref[...] = _rms(x_ref[...], w_ref[...]).astype(o_ref.dtype)


def _rms_cast(x, w, l):
    m, d = x.shape
    tm = _pick(m, 256)
    return pl.pallas_call(
        _rms_cast_kernel,
        grid=(m // tm,),
        in_specs=[pl.BlockSpec((tm, d), lambda i: (i, 0)), _lw((1, d), l)],
        out_specs=pl.BlockSpec((tm, d), lambda i: (i, 0)),
        out_shape=jax.ShapeDtypeStruct((m, d), BF16),
        compiler_params=_params("parallel"),
        name="rms_cast",
    )(x, w)


def _mm_kernel(a_ref, w_ref, o_ref):
    o_ref[...] = _bdot(a_ref[...], w_ref[...]).astype(o_ref.dtype)


def _matmul(a, w, l, ncols, tn, out_dtype, name):
    m, k = a.shape
    tm = _pick(m, 1024)
    return pl.pallas_call(
        _mm_kernel,
        grid=(m // tm, ncols // tn),
        in_specs=[pl.BlockSpec((tm, k), lambda i, j: (i, 0)), pl.BlockSpec((None, k, tn), lambda i, j: (l, 0, j))],
        out_specs=pl.BlockSpec((tm, tn), lambda i, j: (i, j)),
        out_shape=jax.ShapeDtypeStruct((m, ncols), out_dtype),
        compiler_params=_params("parallel", "arbitrary"),
        name=name,
    )(a, w)


def _merge_kernel(h_ref, ya_ref, yb_ref, yc_ref, yd_ref, g0, g1, g2, g3, b0, b1, b2, b3, o_ref):
    h = h_ref[...]
    acc = None
    for y_ref, g_ref, b_ref in ((ya_ref, g0, b0), (yb_ref, g1, b1), (yc_ref, g2, b2), (yd_ref, g3, b3)):
        term = _sigmoid(_bdot(h, g_ref[...])) * _bdot(y_ref[...], b_ref[...])
        acc = term if acc is None else acc + term
    o_ref[...] = acc.astype(o_ref.dtype)


def _merge(h, ys, w_all, wb, l):
    m = h.shape[0]
    tm = _pick(m, 512)
    tn = 256
    y_spec = pl.BlockSpec((tm, BRANCH_W), lambda i, j: (i, 0))
    gate0 = PACK_W // tn
    g_specs = [pl.BlockSpec((None, D_MODEL, tn), functools.partial(lambda i, j, n: (l, 0, gate0 + n * (D_MODEL // tn) + j), n=n))
               for n in range(N_BRANCH)]
    b_specs = [pl.BlockSpec((None, None, BRANCH_W, tn), functools.partial(lambda i, j, n: (l, n, 0, j), n=n)) for n in range(N_BRANCH)]
    return pl.pallas_call(
        _merge_kernel,
        grid=(m // tm, D_MODEL // tn),
        in_specs=[pl.BlockSpec((tm, D_MODEL), lambda i, j: (i, 0))] + [y_spec] * 4 + g_specs + b_specs,
        out_specs=pl.BlockSpec((tm, tn), lambda i, j: (i, j)),
        out_shape=jax.ShapeDtypeStruct((m, D_MODEL), BF16),
        compiler_params=_params("parallel", "arbitrary"),
        name="gate_merge",
    )(h, *ys, w_all, w_all, w_all, w_all, wb, wb, wb, wb)


def _post_kernel(x_ref, o_ref, pw_ref, nw_ref, xn_ref, hn_ref):
    xn = x_ref[...] + _rms(o_ref[...], pw_ref[...])
    xn_ref[...] = xn
    hn_ref[...] = _rms(xn, nw_ref[...]).astype(hn_ref.dtype)


def _post_last_kernel(x_ref, o_ref, pw_ref, xn_ref):
    xn_ref[...] = x_ref[...] + _rms(o_ref[...], pw_ref[...])


def _post(x, out, post_w, pre_w, l, last):
    m, d = x.shape
    tm = _pick(m, 256)
    row = pl.BlockSpec((tm, d), lambda i: (i, 0))
    if last:
        return pl.pallas_call(
            _post_last_kernel, grid=(m // tm,), in_specs=[row, row, _lw((1, d), l)], out_specs=row,
            out_shape=jax.ShapeDtypeStruct((m, d), F32), compiler_params=_params("parallel"), name="post_last",
        )(x, out, post_w), None
    return pl.pallas_call(
        _post_kernel, grid=(m // tm,), in_specs=[row, row, _lw((1, d), l), _lw((1, d), l + 1)], out_specs=[row, row],
        out_shape=[jax.ShapeDtypeStruct((m, d), F32), jax.ShapeDtypeStruct((m, d), BF16)],
        compiler_params=_params("parallel"), name="post",
    )(x, out, post_w, pre_w)


def _conv_tile(ext_s, x, w_ref, width, tl):
    ext_s[8:8 + tl, :] = x
    k0 = 9 - width
    y = ext_s[k0:k0 + tl, :] * w_ref[0:1, :]
    for j in range(1, width - 1):
        y = y + ext_s[k0 + j:k0 + j + tl, :] * w_ref[j:j + 1, :]
    return y + x * w_ref[width - 1:width, :]


def _conv_carry(ext_s, width, tl):
    tail = ext_s[tl + 9 - width:tl + 8, :]
    ext_s[9 - width:8, :] = tail
    return tail


def _lru_kernel(xa_ref, ga_ref, cprev_ref, h0_ref, cw_ref, cb_ref, wr_ref, br_ref, wi_ref, bi_ref, lam_ref,
                y_ref, hlast_ref, cnew_ref, ext_s, a_s, u_s, h_s, *, tl, nl):
    l = pl.program_id(1)

    @pl.when(l == 0)
    def _():
        ext_s[5:8, :] = cprev_ref[...]
        h_s[...] = h0_ref[...]

    xc = _conv_tile(ext_s, xa_ref[...], cw_ref, 4, tl) + cb_ref[...]
    tail = _conv_carry(ext_s, 4, tl)

    xcb = xc.astype(BF16)
    rs, gs = [], []
    for n in range(LRU_BLOCKS):
        blk = xcb[:, n * LRU_BLOCK:(n + 1) * LRU_BLOCK]
        rs.append(_bdot(blk, wr_ref[n]))
        gs.append(_bdot(blk, wi_ref[n]))
    r = _sigmoid(jnp.concatenate(rs, axis=1) + br_ref[...])
    gate_i = _sigmoid(jnp.concatenate(gs, axis=1) + bi_ref[...])
    log_a = (-LRU_C * r) * _softplus(-lam_ref[...])
    a_s[...] = jnp.exp(log_a)
    u_s[...] = jnp.sqrt(_neg_expm1(2.0 * log_a)) * (gate_i * xc)

    def step(t, h):
        h = a_s[pl.ds(t, 1), :] * h + u_s[pl.ds(t, 1), :]
        u_s[pl.ds(t, 1), :] = h
        return h

    h = lax.fori_loop(0, tl, step, h_s[...], unroll=8)
    h_s[...] = h
    y_ref[...] = (u_s[...] * _silu(ga_ref[...])).astype(y_ref.dtype)

    @pl.when(l == nl - 1)
    def _():
        hlast_ref[...] = h
        cnew_ref[...] = tail


def _lru(p, cprev, h0, w, l):
    b, seq, _ = p.shape
    tl = _pick(seq, 256)
    nl = seq // tl
    col = lambda c: pl.BlockSpec((None, tl, BRANCH_W), lambda i, t: (i, t, c // BRANCH_W))
    st3 = pl.BlockSpec((None, 3, BRANCH_W), lambda i, t: (i, 0, 0))
    st1 = pl.BlockSpec((None, 1, BRANCH_W), lambda i, t: (i, 0, 0))
    vec = _lw((1, BRANCH_W), l)
    blk = _lw((LRU_BLOCKS, LRU_BLOCK, LRU_BLOCK), l)
    return pl.pallas_call(
        functools.partial(_lru_kernel, tl=tl, nl=nl),
        grid=(b, nl),
        in_specs=[col(C_XA), col(C_GA), st3, st1, _lw((4, BRANCH_W), l), vec, blk, vec, blk, vec, vec],
        out_specs=[pl.BlockSpec((None, tl, BRANCH_W), lambda i, t: (i, t, 0)), st1, st3],
        out_shape=[jax.ShapeDtypeStruct((b, seq, BRANCH_W), BF16), jax.ShapeDtypeStruct((b, 1, BRANCH_W), F32),
                   jax.ShapeDtypeStruct((b, 3, BRANCH_W), F32)],
        scratch_shapes=[pltpu.VMEM((tl + 8, BRANCH_W), F32), pltpu.VMEM((tl, BRANCH_W), F32),
                        pltpu.VMEM((tl, BRANCH_W), F32), pltpu.VMEM((1, BRANCH_W), F32)],
        compiler_params=_params("parallel", "arbitrary"),
        name="rglru",
    )(p, p, cprev, h0, w["conv_a_w"], w["conv_a_b"], w["lru_w_r"], w["lru_b_r"], w["lru_w_i"], w["lru_b_i"], w["lru_lambda"])


def _sconv_kernel(bb_ref, cb_ref, xb_ref, gb_ref, cprev_ref, cw_ref, y_ref, cnew_ref, ext_s, *, tl, nl):
    l = pl.program_id(1)

    @pl.when(l == 0)
    def _():
        ext_s[6:8, :] = cprev_ref[...]

    v = _conv_tile(ext_s, cb_ref[...] * xb_ref[...], cw_ref, 3, tl)
    tail = _conv_carry(ext_s, 3, tl)
    y_ref[...] = (bb_ref[...] * v * _silu(gb_ref[...])).astype(y_ref.dtype)

    @pl.when(l == nl - 1)
    def _():
        cnew_ref[...] = tail


def _sconv(p, cprev, w, l):
    b, seq, _ = p.shape
    tl = _pick(seq, 256)
    nl = seq // tl
    col = lambda c: pl.BlockSpec((None, tl, BRANCH_W), lambda i, t: (i, t, c // BRANCH_W))
    st2 = pl.BlockSpec((None, 2, BRANCH_W), lambda i, t: (i, 0, 0))
    return pl.pallas_call(
        functools.partial(_sconv_kernel, tl=tl, nl=nl),
        grid=(b, nl),
        in_specs=[col(C_BB), col(C_CB), col(C_XB), col(C_GB), st2, _lw((3, BRANCH_W), l)],
        out_specs=[pl.BlockSpec((None, tl, BRANCH_W), lambda i, t: (i, t, 0)), st2],
        out_shape=[jax.ShapeDtypeStruct((b, seq, BRANCH_W), BF16), jax.ShapeDtypeStruct((b, 2, BRANCH_W), F32)],
        scratch_shapes=[pltpu.VMEM((tl + 8, BRANCH_W), F32)],
        compiler_params=_params("parallel", "arbitrary"),
        name="sconv",
    )(p, p, p, p, cprev, w["conv_b_w"])


def _ssd_kernel(xc_ref, bcc_ref, dt_ref, zc_ref, cpx_ref, cpb_ref, st0_ref, cwx_ref, cbx_ref, cwb_ref, cbb_ref,
                dtb_ref, alog_ref, dexp_ref, nw_ref, e_ref,
                y_ref, cnx_ref, cnb_ref, st_ref, extx_s, extb_s, st_s, *, q, nl):
    l = pl.program_id(1)

    @pl.when(l == 0)
    def _():
        extx_s[5:8, :] = cpx_ref[...]
        extb_s[5:8, :] = cpb_ref[...]
        st_s[...] = st0_ref[...]

    xs = _silu(_conv_tile(extx_s, xc_ref[...], cwx_ref, 4, q) + cbx_ref[...])
    tailx = _conv_carry(extx_s, 4, q)
    bcs = _silu(_conv_tile(extb_s, bcc_ref[...], cwb_ref, 4, q) + cbb_ref[...])
    tailb = _conv_carry(extb_s, 4, q)

    dt = _softplus(dt_ref[...] + dtb_ref[...])
    d_a = dt * (-jnp.exp(alog_ref[...]))
    rows = lax.broadcasted_iota(jnp.int32, (q, q), 0)
    cols = lax.broadcasted_iota(jnp.int32, (q, q), 1)
    tri = rows >= cols
    acum = jnp.dot(tri.astype(F32), d_a, precision=HI, preferred_element_type=F32)
    eye = (lax.broadcasted_iota(jnp.int32, (LANE, LANE), 0) == lax.broadcasted_iota(jnp.int32, (LANE, LANE), 1))
    acum_t = lax.dot_general(eye.astype(F32), acum, NT_DIMS, precision=HI, preferred_element_type=F32)
    expand = e_ref[...]
    dt_x = jnp.dot(dt, expand, precision=HI, preferred_element_type=F32)
    acum_x = jnp.dot(acum, expand, precision=HI, preferred_element_type=F32)
    end_x = acum_x[q - 1:q, :]
    xdt = xs * dt_x
    xw = (xdt * jnp.exp(end_x - acum_x)).astype(BF16)
    xdt_b = xdt.astype(BF16)
    eacum_x = jnp.exp(acum_x)
    cdecay_x = jnp.exp(end_x)
    eye_b = eye.astype(BF16)

    y_parts = []
    for g in range(N_GROUPS_C):
        bg = bcs[:, g * D_STATE:(g + 1) * D_STATE].astype(BF16)
        cg = bcs[:, (N_GROUPS_C + g) * D_STATE:(N_GROUPS_C + g + 1) * D_STATE].astype(BF16)
        cb = lax.dot_general(cg, bg, NT_DIMS, preferred_element_type=F32)
        bg_t = lax.dot_general(eye_b, bg, NT_DIMS, preferred_element_type=F32).astype(BF16)
        for e in range(N_HEADS_C // N_GROUPS_C):
            h = g * (N_HEADS_C // N_GROUPS_C) + e
            lo = h * HEAD_DIM_C
            seg = acum_x[:, lo:lo + q] - acum_t[h:h + 1, :]
            decay = jnp.exp(jnp.where(tri, seg, -jnp.inf))
            y_diag = _bdot((cb * decay).astype(BF16), xdt_b[:, lo:lo + HEAD_DIM_C])
            prev = st_s[h]
            y_off = _bdot(cg, prev.astype(BF16)) * eacum_x[:, lo:lo + HEAD_DIM_C]
            st_s[h] = prev * cdecay_x[:, lo:lo + HEAD_DIM_C] + _bdot(bg_t, xw[:, lo:lo + HEAD_DIM_C])
            y_parts.append(y_diag + y_off)
    y = jnp.concatenate(y_parts, axis=1) + dexp_ref[...] * xs

    gated = y * _silu(zc_ref[...])
    half = BRANCH_W // N_GROUPS_C
    normed = jnp.concatenate(
        [gated[:, i * half:(i + 1) * half]
         * lax.rsqrt(jnp.mean(gated[:, i * half:(i + 1) * half] ** 2, axis=-1, keepdims=True) + EPS)
         for i in range(N_GROUPS_C)], axis=1)
    y_ref[...] = (normed * nw_ref[...]).astype(y_ref.dtype)

    @pl.when(l == nl - 1)
    def _():
        cnx_ref[...] = tailx
        cnb_ref[...] = tailb
        st_ref[...] = st_s[...]


def _ssd(p, cprev_x, cprev_b, st0, w, l):
    b, seq, _ = p.shape
    q = CHUNK if seq % CHUNK == 0 else seq
    nl = seq // q
    st_shape = (N_HEADS_C, D_STATE, HEAD_DIM_C)
    colx = lambda c, wd: pl.BlockSpec((None, q, wd), lambda i, t: (i, t, c // wd))
    st3 = lambda wd: pl.BlockSpec((None, 3, wd), lambda i, t: (i, 0, 0))
    st_spec = pl.BlockSpec((None,) + st_shape, lambda i, t: (i, 0, 0, 0))
    return pl.pallas_call(
        functools.partial(_ssd_kernel, q=q, nl=nl),
        grid=(b, nl),
        in_specs=[colx(C_XC, BRANCH_W), colx(C_BCC, BC_W), colx(C_DT, LANE), colx(C_ZC, BRANCH_W),
                  st3(BRANCH_W), st3(BC_W), st_spec,
                  _lw((4, BRANCH_W), l), _lw((1, BRANCH_W), l), _lw((4, BC_W), l), _lw((1, BC_W), l),
                  _lw((1, LANE), l), _lw((1, LANE), l), _lw((1, BRANCH_W), l), _lw((1, BRANCH_W), l),
                  pl.BlockSpec((LANE, BRANCH_W), lambda i, t: (0, 0))],
        out_specs=[pl.BlockSpec((None, q, BRANCH_W), lambda i, t: (i, t, 0)), st3(BRANCH_W), st3(BC_W), st_spec],
        out_shape=[jax.ShapeDtypeStruct((b, seq, BRANCH_W), BF16), jax.ShapeDtypeStruct((b, 3, BRANCH_W), F32),
                   jax.ShapeDtypeStruct((b, 3, BC_W), F32), jax.ShapeDtypeStruct((b,) + st_shape, F32)],
        scratch_shapes=[pltpu.VMEM((q + 8, BRANCH_W), F32), pltpu.VMEM((q + 8, BC_W), F32), pltpu.VMEM(st_shape, F32)],
        compiler_params=_params("parallel", "arbitrary"),
        name="ssd",
    )(p, p, p, p, cprev_x, cprev_b, st0, w["conv_cx_w"], w["conv_cx_b"], w["conv_cb_w"], w["conv_cb_b"],
      w["dt_bias"], w["a_log"], w["d_exp"], w["ssd_norm_w"], w["expand"])


def _mla_prep_kernel(ql_ref, kvl_ref, kraw_ref, cq_ref, sq_ref, ck_ref, sk_ref, qnw_ref, kvnw_ref, wq_ref, wqs_ref,
                     q_ref, ckv_ref, kpe_ref, kpe128_ref):
    qn = _rms(ql_ref[...], qnw_ref[...]).astype(BF16)
    q = _bdot(qn, wq_ref[...])
    q_sw = _bdot(qn, wqs_ref[...])
    cos_q, sin_q = cq_ref[...], sq_ref[...]
    for h in range(N_HEADS_D):
        sl = slice(h * Q_SLAB, (h + 1) * Q_SLAB)
        q_ref[:, sl] = (q[:, sl] * cos_q + q_sw[:, sl] * sin_q).astype(q_ref.dtype)
    ckv_ref[...] = _rms(kvl_ref[...], kvnw_ref[...])
    kraw = kraw_ref[...]
    lane = lax.broadcasted_iota(jnp.int32, kraw.shape, 1)
    swapped = jnp.where(lane < HALF, pltpu.roll(kraw, LANE - HALF, axis=1), pltpu.roll(kraw, HALF, axis=1))
    kr = kraw * ck_ref[...] + swapped * sk_ref[...]
    kpe128_ref[...] = kr
    kpe_ref[...] = kr[:, :ROPE_DIM]


def _mla_prep(p2d, tabs, w, l):
    m = p2d.shape[0]
    tm = _pick(m, 256)
    col = lambda c, wd: pl.BlockSpec((tm, wd), lambda i: (i, c // wd))
    row = lambda wd: pl.BlockSpec((tm, wd), lambda i: (i, 0))
    qw = N_HEADS_D * Q_SLAB
    return pl.pallas_call(
        _mla_prep_kernel,
        grid=(m // tm,),
        in_specs=[col(C_QL, Q_LORA), col(C_KVL, KV_LORA), col(C_KR, LANE),
                  row(Q_SLAB), row(Q_SLAB), row(LANE), row(LANE),
                  _lw((1, Q_LORA), l), _lw((1, KV_LORA), l), _lw((Q_LORA, qw), l), _lw((Q_LORA, qw), l)],
        out_specs=[row(qw), row(KV_LORA), row(ROPE_DIM), row(LANE)],
        out_shape=[jax.ShapeDtypeStruct((m, qw), BF16), jax.ShapeDtypeStruct((m, KV_LORA), F32),
                   jax.ShapeDtypeStruct((m, ROPE_DIM), F32), jax.ShapeDtypeStruct((m, LANE), F32)],
        compiler_params=_params("parallel"),
        name="mla_prep",
    )(p2d, p2d, p2d, tabs["cos_q"], tabs["sin_q"], tabs["cos_k"], tabs["sin_k"],
      w["mla_q_norm_w"], w["mla_kv_norm_w"], w["wq"], w["wq_sw"])


def _kv_expand_kernel(ckv_ref, kpe_ref, wk_ref, wv_ref, k_ref, v_ref):
    c = ckv_ref[...].astype(BF16)
    kn = _bdot(c, wk_ref[...])
    v_ref[...] = _bdot(c, wv_ref[...]).astype(v_ref.dtype)
    kr = kpe_ref[...]
    for h in range(N_HEADS_D):
        k_ref[:, h * Q_SLAB:h * Q_SLAB + NOPE_DIM] = kn[:, h * NOPE_DIM:(h + 1) * NOPE_DIM].astype(k_ref.dtype)
        k_ref[:, h * Q_SLAB + NOPE_DIM:(h + 1) * Q_SLAB] = kr.astype(k_ref.dtype)


def _kv_expand(ckv, kpe128, w, l):
    r = ckv.shape[0]
    tr = _pick(r, 512)
    row = lambda wd: pl.BlockSpec((tr, wd), lambda i: (i, 0))
    kw, vw = N_HEADS_D * Q_SLAB, N_HEADS_D * V_DIM
    return pl.pallas_call(
        _kv_expand_kernel,
        grid=(r // tr,),
        in_specs=[row(KV_LORA), row(LANE), _lw((KV_LORA, N_HEADS_D * NOPE_DIM), l), _lw((KV_LORA, vw), l)],
        out_specs=[row(kw), row(vw)],
        out_shape=[jax.ShapeDtypeStruct((r, kw), BF16), jax.ShapeDtypeStruct((r, vw), BF16)],
        compiler_params=_params("parallel"),
        name="kv_expand",
    )(ckv, kpe128, w["wk"], w["wv"])


def _attn_kernel(qi_ref, kj_ref, fin_ref, q_ref, k_ref, v_ref, gd_ref, o_ref, m_s, l_s, acc_s, s_a, s_b, p_a, p_b,
                 *, tq, tk, strip, q_off, lk):
    step = pl.program_id(1)
    j = kj_ref[step]
    q_lo = q_off + qi_ref[step] * tq
    k_lo = j * tk
    all_visible = jnp.minimum((q_lo // CHUNK + 1) * CHUNK, lk)
    s_bufs, p_bufs = (s_a, s_b), (p_a, p_b)

    @pl.when(j == 0)
    def _():
        m_s[...] = jnp.full(m_s.shape, NEG_BIG, F32)
        l_s[...] = jnp.zeros(l_s.shape, F32)
        acc_s[...] = jnp.zeros(acc_s.shape, F32)

    def block(masked):
        if masked:
            qpos = q_lo + lax.broadcasted_iota(jnp.int32, (tq, tk), 0)
            kpos = k_lo + lax.broadcasted_iota(jnp.int32, (tq, tk), 1)
            visible = kpos < jnp.minimum((qpos // CHUNK + 1) * CHUNK, lk)
        def scores(h):
            q = q_ref[:, h * Q_SLAB:(h + 1) * Q_SLAB]
            k = k_ref[:, h * Q_SLAB:(h + 1) * Q_SLAB]
            s = lax.dot_general(q, k, NT_DIMS, preferred_element_type=F32)
            if masked:
                s = jnp.where(visible, s, NEG_BIG)
            s_bufs[h % 2][...] = s
            m_prev = m_s[h]
            m_next = jnp.maximum(m_prev, jnp.max(s, axis=1, keepdims=True))
            m_s[h] = m_next
            return m_next, jnp.exp2((m_prev - m_next) * EXP2_SCALE)

        def finish(h, m_next, alpha):
            s_buf, p_buf = s_bufs[h % 2], p_bufs[h % 2]
            for r in range(tq // strip):
                rows = slice(r * strip, (r + 1) * strip)
                pr = jnp.exp2((s_buf[rows, :] - m_next[rows, 0:1]) * EXP2_SCALE)
                part = pr[:, 0:LANE]
                for c in range(1, tk // LANE):
                    part = part + pr[:, c * LANE:(c + 1) * LANE]
                l_s[h, rows, :] = alpha[rows] * l_s[h, rows, :] + part
                p_buf[rows, :] = pr.astype(BF16)
            acc_s[h] = acc_s[h] * alpha + _bdot(p_buf[...], v_ref[:, h * V_DIM:(h + 1) * V_DIM])

        stats = scores(0)
        for h in range(N_HEADS_D):
            nxt = scores(h + 1) if h + 1 < N_HEADS_D else None
            finish(h, *stats)
            stats = nxt

    @pl.when(k_lo + tk <= all_visible)
    def _():
        block(False)

    @pl.when(k_lo + tk > all_visible)
    def _():
        block(True)

    @pl.when(fin_ref[step] == 1)
    def _():
        gate = _silu(gd_ref[...])
        for h in range(N_HEADS_D):
            sl = slice(h * V_DIM, (h + 1) * V_DIM)
            denom = jnp.sum(l_s[h], axis=1, keepdims=True)
            o_ref[:, sl] = (acc_s[h] / denom * gate[:, sl]).astype(o_ref.dtype)


def _attention(q, k, v, p, q_off, lk):
    b, lq, _ = q.shape
    lkp = k.shape[1]
    tq = _pick(lq, 512)
    tk = 1024 if lkp % 1024 == 0 else lkp
    strip = 16 if tq % 16 == 0 else SUBLANE
    nq = lq // tq
    qi, kj, fin = [], [], []
    for i in range(nq):
        last_bound = min(((q_off + (i + 1) * tq - 1) // CHUNK + 1) * CHUNK, lk)
        last_j = (last_bound - 1) // tk
        for j in range(last_j + 1):
            qi.append(i)
            kj.append(j)
            fin.append(int(j == last_j))
    to_arr = lambda a: jnp.asarray(np.asarray(a, np.int32))
    qw, vw = N_HEADS_D * Q_SLAB, N_HEADS_D * V_DIM
    grid_spec = pltpu.PrefetchScalarGridSpec(
        num_scalar_prefetch=3,
        grid=(b, len(qi)),
        in_specs=[pl.BlockSpec((None, tq, qw), lambda bi, s, qi_r, kj_r, fin_r: (bi, qi_r[s], 0)),
                  pl.BlockSpec((None, tk, qw), lambda bi, s, qi_r, kj_r, fin_r: (bi, kj_r[s], 0)),
                  pl.BlockSpec((None, tk, vw), lambda bi, s, qi_r, kj_r, fin_r: (bi, kj_r[s], 0)),
                  pl.BlockSpec((None, tq, vw), lambda bi, s, qi_r, kj_r, fin_r: (bi, qi_r[s], C_GD // vw))],
        out_specs=pl.BlockSpec((None, tq, vw), lambda bi, s, qi_r, kj_r, fin_r: (bi, qi_r[s], 0)),
        scratch_shapes=[pltpu.VMEM((N_HEADS_D, tq, LANE), F32), pltpu.VMEM((N_HEADS_D, tq, LANE), F32),
                        pltpu.VMEM((N_HEADS_D, tq, V_DIM), F32), pltpu.VMEM((tq, tk), F32), pltpu.VMEM((tq, tk), F32),
                        pltpu.VMEM((tq, tk), BF16), pltpu.VMEM((tq, tk), BF16)],
    )
    return pl.pallas_call(
        functools.partial(_attn_kernel, tq=tq, tk=tk, strip=strip, q_off=q_off, lk=lk),
        grid_spec=grid_spec,
        out_shape=jax.ShapeDtypeStruct((b, lq, vw), BF16),
        compiler_params=_params("parallel", "arbitrary"),
        name="mla_attention",
    )(to_arr(qi), to_arr(kj), to_arr(fin), q, k, v, p)


def _prep_weights(pre_norm_w, w_in, conv_a_w, conv_a_b, lru_w_r, lru_b_r, lru_w_i, lru_b_i, lru_lambda, conv_b_w,
                  conv_c_w, conv_c_b, ssd_dt_bias, ssd_a_log, ssd_d, ssd_norm_w, mla_q_norm_w, mla_w_q_up,
                  mla_kv_norm_w, mla_w_kv_up, w_branch_out, w_out, post_norm_w):
    d = w_in.shape[0]
    w_all = _pack_w_in(w_in)

    wq = mla_w_q_up.reshape(d, Q_LORA, N_HEADS_D, NOPE_DIM + ROPE_DIM)
    nope, x1, x2 = wq[..., :NOPE_DIM], wq[..., NOPE_DIM:NOPE_DIM + HALF], wq[..., NOPE_DIM + HALF:]
    zq = lambda n: jnp.zeros((d, Q_LORA, N_HEADS_D, n), wq.dtype)
    wq_main = jnp.concatenate([nope, x1, x2, zq(Q_SLAB - NOPE_DIM - ROPE_DIM)], -1).reshape(d, Q_LORA, -1).astype(BF16)
    wq_sw = jnp.concatenate([zq(NOPE_DIM), -x2, x1, zq(Q_SLAB - NOPE_DIM - ROPE_DIM)], -1).reshape(d, Q_LORA, -1).astype(BF16)
    wkv = mla_w_kv_up.reshape(d, KV_LORA, N_HEADS_D, NOPE_DIM + V_DIM)
    wk = wkv[..., :NOPE_DIM].reshape(d, KV_LORA, -1).astype(BF16)
    wv = wkv[..., NOPE_DIM:].reshape(d, KV_LORA, -1).astype(BF16)

    pad_heads = lambda a: jnp.pad(a, ((0, 0), (0, LANE - N_HEADS_C)))[:, None, :]
    expand = (jnp.arange(LANE)[:, None] == (jnp.arange(BRANCH_W) // HEAD_DIM_C)[None, :]).astype(F32)
    vec = lambda a: a[:, None, :]
    return dict(
        pre_norm_w=vec(pre_norm_w), post_norm_w=vec(post_norm_w), w_all=w_all,
        w_branch=w_branch_out.astype(BF16), w_out=w_out.astype(BF16),
        conv_a_w=conv_a_w, conv_a_b=vec(conv_a_b), lru_w_r=lru_w_r.astype(BF16), lru_b_r=vec(lru_b_r),
        lru_w_i=lru_w_i.astype(BF16), lru_b_i=vec(lru_b_i), lru_lambda=vec(lru_lambda), conv_b_w=conv_b_w,
        conv_cx_w=conv_c_w[..., :BRANCH_W], conv_cx_b=vec(conv_c_b[..., :BRANCH_W]),
        conv_cb_w=conv_c_w[..., BRANCH_W:], conv_cb_b=vec(conv_c_b[..., BRANCH_W:]),
        dt_bias=pad_heads(ssd_dt_bias), a_log=pad_heads(ssd_a_log), d_exp=vec(jnp.repeat(ssd_d, HEAD_DIM_C, axis=-1)),
        ssd_norm_w=vec(ssd_norm_w), expand=expand,
        mla_q_norm_w=vec(mla_q_norm_w), mla_kv_norm_w=vec(mla_kv_norm_w), wq=wq_main, wq_sw=wq_sw, wk=wk, wv=wv,
    )


def _rope_tables(pos, reps):
    inv = ROPE_THETA ** (-jnp.arange(HALF, dtype=F32) / HALF)
    ang = pos.astype(F32)[:, None] * inv[None, :]
    cos, sin = jnp.cos(ang), jnp.sin(ang)
    n = pos.shape[0]
    z = lambda w: jnp.zeros((n, w), F32)
    tabs = dict(
        cos_q=jnp.concatenate([jnp.ones((n, NOPE_DIM), F32), cos, cos, z(Q_SLAB - NOPE_DIM - ROPE_DIM)], 1),
        sin_q=jnp.concatenate([z(NOPE_DIM), sin, sin, z(Q_SLAB - NOPE_DIM - ROPE_DIM)], 1),
        cos_k=jnp.concatenate([cos, cos, z(LANE - ROPE_DIM)], 1),
        sin_k=jnp.concatenate([-sin, sin, z(LANE - ROPE_DIM)], 1),
    )
    return {k: jnp.tile(v, (reps, 1)) for k, v in tabs.items()}


def _layer(x, h, b, seq, tabs, q_off, state, past, w, l, last):
    conv_a_prev, h0, conv_b_prev, conv_c_prev, ssd0 = state
    m = b * seq
    p2d = _matmul(h, w["w_all"], l, PACK_W, 1024, F32, "in_proj")
    p = p2d.reshape(b, seq, PACK_W)

    y_a, h_last, conv_a_new = _lru(p, conv_a_prev, h0[:, None, :], w, l)
    y_b, conv_b_new = _sconv(p, conv_b_prev, w, l)
    y_c, conv_cx_new, conv_cb_new, ssd_t = _ssd(p, conv_c_prev[..., :BRANCH_W], conv_c_prev[..., BRANCH_W:],
                                                 jnp.swapaxes(ssd0, -1, -2), w, l)
    q, ckv, kpe, kpe128 = _mla_prep(p2d, tabs, w, l)
    if past is None:
        ckv_all, kpe_all, lk, lkp = ckv, kpe128, seq, seq
    else:
        past_ckv, past_kpe = past
        lk = past_ckv.shape[1] + seq
        lkp = -(-lk // LANE) * LANE
        tail = lambda wd: jnp.zeros((b, lkp - lk, wd), F32)
        ckv_all = jnp.concatenate([past_ckv, ckv.reshape(b, seq, KV_LORA), tail(KV_LORA)], axis=1).reshape(b * lkp, KV_LORA)
        past_kpe128 = jnp.pad(past_kpe, ((0, 0), (0, 0), (0, LANE - ROPE_DIM)))
        kpe_all = jnp.concatenate([past_kpe128, kpe128.reshape(b, seq, LANE), tail(LANE)], axis=1).reshape(b * lkp, LANE)
    k, v = _kv_expand(ckv_all, kpe_all, w, l)
    y_d = _attention(q.reshape(b, seq, -1), k.reshape(b, lkp, -1), v.reshape(b, lkp, -1), p, q_off, lk)

    merged = _merge(h, [y.reshape(m, BRANCH_W) for y in (y_a, y_b, y_c, y_d)], w["w_all"], w["w_branch"], l)
    out = _matmul(merged, w["w_out"], l, D_MODEL, 1024, F32, "out_proj")
    x_new, h_next = _post(x, out, w["post_norm_w"], w["pre_norm_w"], l, last)
    new_state = (ckv.reshape(b, seq, KV_LORA), kpe.reshape(b, seq, ROPE_DIM), conv_a_new, h_last[:, 0, :], conv_b_new,
                 jnp.concatenate([conv_cx_new, conv_cb_new], axis=-1), jnp.swapaxes(ssd_t, -1, -2))
    return x_new, h_next, new_state


def kernel(x_prompt, x_sample, cache_mla_latent, cache_mla_kpe, state_lru_conv, state_lru_h, state_sconv, state_ssd_conv, state_ssd, pre_norm_w, w_in, conv_a_w, conv_a_b, lru_w_r, lru_b_r, lru_w_i, lru_b_i, lru_lambda, conv_b_w, conv_c_w, conv_c_b, ssd_dt_bias, ssd_a_log, ssd_d, ssd_norm_w, mla_q_norm_w, mla_w_q_up, mla_kv_norm_w, mla_w_kv_up, w_branch_out, w_out, post_norm_w):
    bp, sp, d = x_prompt.shape
    bs, ss, _ = x_sample.shape
    past = cache_mla_latent.shape[2]
    depth = w_in.shape[0]
    w = _prep_weights(pre_norm_w, w_in, conv_a_w, conv_a_b, lru_w_r, lru_b_r, lru_w_i, lru_b_i, lru_lambda, conv_b_w,
                      conv_c_w, conv_c_b, ssd_dt_bias, ssd_a_log, ssd_d, ssd_norm_w, mla_q_norm_w, mla_w_q_up,
                      mla_kv_norm_w, mla_w_kv_up, w_branch_out, w_out, post_norm_w)
    tabs_p = _rope_tables(jnp.arange(sp), bp)
    tabs_s = _rope_tables(past + jnp.arange(ss), bs)
    zeros_p = (jnp.zeros((bp, 3, BRANCH_W), F32), jnp.zeros((bp, BRANCH_W), F32), jnp.zeros((bp, 2, BRANCH_W), F32),
               jnp.zeros((bp, 3, BRANCH_W + BC_W), F32), jnp.zeros((bp, N_HEADS_C, HEAD_DIM_C, D_STATE), F32))

    xp = x_prompt.reshape(bp * sp, d)
    xs = x_sample.reshape(bs * ss, d)
    hp = _rms_cast(xp, w["pre_norm_w"], 0)
    hs = _rms_cast(xs, w["pre_norm_w"], 0)
    p_st = [[] for _ in range(7)]
    s_st = [[] for _ in range(7)]
    for l in range(depth):
        last = l == depth - 1
        xp, hp, new_p = _layer(xp, hp, bp, sp, tabs_p, 0, zeros_p, None, w, l, last)
        st_s = (state_lru_conv[l], state_lru_h[l], state_sconv[l], state_ssd_conv[l], state_ssd[l])
        xs, hs, new_s = _layer(xs, hs, bs, ss, tabs_s, past, st_s, (cache_mla_latent[l], cache_mla_kpe[l]), w, l, last)
        for i in range(7):
            p_st[i].append(new_p[i])
            s_st[i].append(new_s[i])
    p_out = [jnp.stack(v, axis=0) for v in p_st]
    s_out = [jnp.stack(v, axis=0) for v in s_st]
    return (xp.reshape(bp, sp, d), xs.reshape(bs, ss, d), *p_out, *s_out)
```

```python
import functools

import numpy as np
import jax
import jax.numpy as jnp
from jax import lax
from jax.experimental import pallas as pl
from jax.experimental.pallas import tpu as pltpu

F32 = jnp.float32
BF16 = jnp.bfloat16

D_MODEL = 4096
CHUNK = 64
BRANCH_W = 1024
N_BRANCH = 4
EPS = 1e-6
LRU_BLOCKS = 8
LRU_BLOCK = 128
LRU_C = 8.0
N_HEADS_C = 16
HEAD_DIM_C = 64
N_GROUPS_C = 2
D_STATE = 128
N_HEADS_D = 8
NOPE_DIM = 128
ROPE_DIM = 64
V_DIM = 128
Q_LORA = 1024
KV_LORA = 512
ROPE_THETA = 10000.0
ATTN_SCALE = (NOPE_DIM + ROPE_DIM) ** -0.5
HALF = ROPE_DIM // 2
BC_W = 2 * N_GROUPS_C * D_STATE

LANE = 128
SUBLANE = 8
VMEM_LIMIT = 56 * 1024 * 1024

O_BC = 7168
O_DT = 7680
O_ZC = 7696
O_QL = 8720
O_KVL = 9744
O_KPE = 10256
O_GD = 10320
O_MG = 11344
PACK_TILE = 512
PACK_SRC = (tuple(range(0, O_DT, PACK_TILE)) + (O_DT,) + (O_ZC, O_ZC + PACK_TILE) + (O_QL, O_QL + PACK_TILE)
            + (O_KVL, O_KPE) + (O_GD, O_GD + PACK_TILE))
PACK_W = len(PACK_SRC) * PACK_TILE
C_XA, C_GA, C_BB, C_CB, C_XB, C_GB, C_XC = (i * BRANCH_W for i in range(7))
C_BCC = 7168
C_DT = 7680
C_ZC = 8192
C_QL = 9216
C_KVL = 10240
C_KR = 10752
C_GD = 11264
Q_SLAB = 256

NT_DIMS = (((1,), (1,)), ((), ()))
HI = lax.Precision.HIGHEST
EXP2_SCALE = ATTN_SCALE * 1.4426950408889634
NEG_BIG = -1e30


def _pick(n, pref):
    if n <= pref:
        return n
    t = pref
    while n % t:
        t -= SUBLANE
    return t


def _params(*sem):
    return pltpu.CompilerParams(dimension_semantics=sem, vmem_limit_bytes=VMEM_LIMIT)


def _lw(shape, l):
    idx = (l,) + (0,) * len(shape)
    return pl.BlockSpec((None,) + tuple(shape), lambda *_: idx)


def _sigmoid(x):
    return 1.0 / (1.0 + jnp.exp(-x))


def _silu(x):
    return x * _sigmoid(x)


def _softplus(x):
    return jnp.maximum(x, 0.0) + jnp.log1p(jnp.exp(-jnp.abs(x)))


def _neg_expm1(y):
    e = jnp.exp(y)
    near = jnp.where(e == 1.0, -y, (1.0 - e) * y / jnp.log(e))
    return jnp.where(y > -0.5, near, 1.0 - e)


def _rms(x, w):
    return x * lax.rsqrt(jnp.mean(x * x, axis=-1, keepdims=True) + EPS) * w


def _bdot(a, b):
    return jnp.dot(a, b, preferred_element_type=F32)


def _sum3(dot_with, x):
    x1 = x.astype(BF16)
    r1 = x - x1.astype(F32)
    x2 = r1.astype(BF16)
    x3 = (r1 - x2.astype(F32)).astype(BF16)
    return dot_with(x1) + dot_with(x2) + dot_with(x3)


def _pack_kernel(off_ref, a_ref, o_ref):
    del off_ref
    o_ref[...] = a_ref[0].T.astype(o_ref.dtype)


def _pack_w_in(w_in):
    d, k, n = w_in.shape
    wt = jnp.swapaxes(w_in, 1, 2)
    src = PACK_SRC + tuple(range(O_MG, n, PACK_TILE))
    grid_spec = pltpu.PrefetchScalarGridSpec(
        num_scalar_prefetch=1,
        grid=(d, len(src)),
        in_specs=[pl.BlockSpec((pl.Element(1), pl.Element(PACK_TILE), pl.Element(k)), lambda l, j, off: (l, pl.multiple_of(off[j], 16), 0))],
        out_specs=pl.BlockSpec((None, k, PACK_TILE), lambda l, j, off: (l, 0, j)),
    )
    return pl.pallas_call(
        _pack_kernel, grid_spec=grid_spec, out_shape=jax.ShapeDtypeStruct((d, k, len(src) * PACK_TILE), BF16),
        compiler_params=_params("parallel", "parallel"), name="pack_w_in",
    )(jnp.asarray(np.asarray(src, np.int32)), wt)


def _rms_cast_kernel(x_ref, w_ref, o_ref):
    o_ref[...] = _rms(x_ref[...], w_ref[...]).astype(o_ref.dtype)


def _rms_cast(x, w, l):
    m, d = x.shape
    tm = _pick(m, 256)
    return pl.pallas_call(
        _rms_cast_kernel,
        grid=(m // tm,),
        in_specs=[pl.BlockSpec((tm, d), lambda i: (i, 0)), _lw((1, d), l)],
        out_specs=pl.BlockSpec((tm, d), lambda i: (i, 0)),
        out_shape=jax.ShapeDtypeStruct((m, d), BF16),
        compiler_params=_params("parallel"),
        name="rms_cast",
    )(x, w)


def _mm_kernel(a_ref, w_ref, o_ref):
    o_ref[...] = _bdot(a_ref[...], w_ref[...]).astype(o_ref.dtype)


def _matmul(a, w, l, ncols, tn, out_dtype, name):
    m, k = a.shape
    tm = _pick(m, 1024)
    return pl.pallas_call(
        _mm_kernel,
        grid=(m // tm, ncols // tn),
        in_specs=[pl.BlockSpec((tm, k), lambda i, j: (i, 0)), pl.BlockSpec((None, k, tn), lambda i, j: (l, 0, j))],
        out_specs=pl.BlockSpec((tm, tn), lambda i, j: (i, j)),
        out_shape=jax.ShapeDtypeStruct((m, ncols), out_dtype),
        compiler_params=_params("parallel", "arbitrary"),
        name=name,
    )(a, w)


def _merge_kernel(h_ref, ya_ref, yb_ref, yc_ref, yd_ref, g0, g1, g2, g3, b0, b1, b2, b3, o_ref):
    h = h_ref[...]
    acc = None
    for y_ref, g_ref, b_ref in ((ya_ref, g0, b0), (yb_ref, g1, b1), (yc_ref, g2, b2), (yd_ref, g3, b3)):
        term = _sigmoid(_bdot(h, g_ref[...])) * _bdot(y_ref[...], b_ref[...])
        acc = term if acc is None else acc + term
    o_ref[...] = acc.astype(o_ref.dtype)


def _merge(h, ys, w_all, wb, l):
    m = h.shape[0]
    tm = _pick(m, 1024)
    tn = 256
    once = pl.Buffered(1)
    y_spec = pl.BlockSpec((tm, BRANCH_W), lambda i, j: (i, 0), pipeline_mode=once)
    gate0 = PACK_W // tn
    g_specs = [pl.BlockSpec((None, D_MODEL, tn), functools.partial(lambda i, j, n: (l, 0, gate0 + n * (D_MODEL // tn) + j), n=n))
               for n in range(N_BRANCH)]
    b_specs = [pl.BlockSpec((None, None, BRANCH_W, tn), functools.partial(lambda i, j, n: (l, n, 0, j), n=n)) for n in range(N_BRANCH)]
    return pl.pallas_call(
        _merge_kernel,
        grid=(m // tm, D_MODEL // tn),
        in_specs=[pl.BlockSpec((tm, D_MODEL), lambda i, j: (i, 0), pipeline_mode=once)] + [y_spec] * 4 + g_specs + b_specs,
        out_specs=pl.BlockSpec((tm, tn), lambda i, j: (i, j)),
        out_shape=jax.ShapeDtypeStruct((m, D_MODEL), BF16),
        compiler_params=_params("parallel", "arbitrary"),
        name="gate_merge",
    )(h, *ys, w_all, w_all, w_all, w_all, wb, wb, wb, wb)


def _post_kernel(x_ref, o_ref, pw_ref, nw_ref, xn_ref, hn_ref):
    xn = x_ref[...] + _rms(o_ref[...], pw_ref[...])
    xn_ref[...] = xn
    hn_ref[...] = _rms(xn, nw_ref[...]).astype(hn_ref.dtype)


def _post_last_kernel(x_ref, o_ref, pw_ref, xn_ref):
    xn_ref[...] = x_ref[...] + _rms(o_ref[...], pw_ref[...])


def _post(x, out, post_w, pre_w, l, last):
    m, d = x.shape
    tm = _pick(m, 256)
    row = pl.BlockSpec((tm, d), lambda i: (i, 0))
    if last:
        return pl.pallas_call(
            _post_last_kernel, grid=(m // tm,), in_specs=[row, row, _lw((1, d), l)], out_specs=row,
            out_shape=jax.ShapeDtypeStruct((m, d), F32), compiler_params=_params("parallel"), name="post_last",
        )(x, out, post_w), None
    return pl.pallas_call(
        _post_kernel, grid=(m // tm,), in_specs=[row, row, _lw((1, d), l), _lw((1, d), l + 1)], out_specs=[row, row],
        out_shape=[jax.ShapeDtypeStruct((m, d), F32), jax.ShapeDtypeStruct((m, d), BF16)],
        compiler_params=_params("parallel"), name="post",
    )(x, out, post_w, pre_w)


def _conv_tile(ext_s, x, w_ref, width, tl):
    ext_s[8:8 + tl, :] = x
    k0 = 9 - width
    y = ext_s[k0:k0 + tl, :] * w_ref[0:1, :]
    for j in range(1, width - 1):
        y = y + ext_s[k0 + j:k0 + j + tl, :] * w_ref[j:j + 1, :]
    return y + x * w_ref[width - 1:width, :]


def _conv_carry(ext_s, width, tl):
    tail = ext_s[tl + 9 - width:tl + 8, :]
    ext_s[9 - width:8, :] = tail
    return tail


def _lru_kernel(xa_ref, ga_ref, cprev_ref, h0_ref, cw_ref, cb_ref, wr_ref, br_ref, wi_ref, bi_ref, lam_ref,
                y_ref, hlast_ref, cnew_ref, ext_s, a_s, u_s, h_s, *, tl, nl):
    l = pl.program_id(1)

    @pl.when(l == 0)
    def _():
        ext_s[5:8, :] = cprev_ref[...]
        h_s[...] = h0_ref[...]

    xc = _conv_tile(ext_s, xa_ref[...], cw_ref, 4, tl) + cb_ref[...]
    tail = _conv_carry(ext_s, 4, tl)

    xcb = xc.astype(BF16)
    rs, gs = [], []
    for n in range(LRU_BLOCKS):
        blk = xcb[:, n * LRU_BLOCK:(n + 1) * LRU_BLOCK]
        rs.append(_bdot(blk, wr_ref[n]))
        gs.append(_bdot(blk, wi_ref[n]))
    r = _sigmoid(jnp.concatenate(rs, axis=1) + br_ref[...])
    gate_i = _sigmoid(jnp.concatenate(gs, axis=1) + bi_ref[...])
    log_a = (-LRU_C * r) * _softplus(-lam_ref[...])
    a_s[...] = jnp.exp(log_a)
    u_s[...] = jnp.sqrt(_neg_expm1(2.0 * log_a)) * (gate_i * xc)

    def step(t, h):
        h = a_s[pl.ds(t, 1), :] * h + u_s[pl.ds(t, 1), :]
        u_s[pl.ds(t, 1), :] = h
        return h

    h = lax.fori_loop(0, tl, step, h_s[...], unroll=8)
    h_s[...] = h
    y_ref[...] = (u_s[...] * _silu(ga_ref[...])).astype(y_ref.dtype)

    @pl.when(l == nl - 1)
    def _():
        hlast_ref[...] = h
        cnew_ref[...] = tail


def _lru(p, cprev, h0, w, l):
    b, seq, _ = p.shape
    tl = _pick(seq, 256)
    nl = seq // tl
    col = lambda c: pl.BlockSpec((None, tl, BRANCH_W), lambda i, t: (i, t, c // BRANCH_W))
    st3 = pl.BlockSpec((None, 3, BRANCH_W), lambda i, t: (i, 0, 0))
    st1 = pl.BlockSpec((None, 1, BRANCH_W), lambda i, t: (i, 0, 0))
    vec = _lw((1, BRANCH_W), l)
    blk = _lw((LRU_BLOCKS, LRU_BLOCK, LRU_BLOCK), l)
    return pl.pallas_call(
        functools.partial(_lru_kernel, tl=tl, nl=nl),
        grid=(b, nl),
        in_specs=[col(C_XA), col(C_GA), st3, st1, _lw((4, BRANCH_W), l), vec, blk, vec, blk, vec, vec],
        out_specs=[pl.BlockSpec((None, tl, BRANCH_W), lambda i, t: (i, t, 0)), st1, st3],
        out_shape=[jax.ShapeDtypeStruct((b, seq, BRANCH_W), BF16), jax.ShapeDtypeStruct((b, 1, BRANCH_W), F32),
                   jax.ShapeDtypeStruct((b, 3, BRANCH_W), F32)],
        scratch_shapes=[pltpu.VMEM((tl + 8, BRANCH_W), F32), pltpu.VMEM((tl, BRANCH_W), F32),
                        pltpu.VMEM((tl, BRANCH_W), F32), pltpu.VMEM((1, BRANCH_W), F32)],
        compiler_params=_params("parallel", "arbitrary"),
        name="rglru",
    )(p, p, cprev, h0, w["conv_a_w"], w["conv_a_b"], w["lru_w_r"], w["lru_b_r"], w["lru_w_i"], w["lru_b_i"], w["lru_lambda"])


def _sconv_kernel(bb_ref, cb_ref, xb_ref, gb_ref, cprev_ref, cw_ref, y_ref, cnew_ref, ext_s, *, tl, nl):
    l = pl.program_id(1)

    @pl.when(l == 0)
    def _():
        ext_s[6:8, :] = cprev_ref[...]

    v = _conv_tile(ext_s, cb_ref[...] * xb_ref[...], cw_ref, 3, tl)
    tail = _conv_carry(ext_s, 3, tl)
    y_ref[...] = (bb_ref[...] * v * _silu(gb_ref[...])).astype(y_ref.dtype)

    @pl.when(l == nl - 1)
    def _():
        cnew_ref[...] = tail


def _sconv(p, cprev, w, l):
    b, seq, _ = p.shape
    tl = _pick(seq, 256)
    nl = seq // tl
    col = lambda c: pl.BlockSpec((None, tl, BRANCH_W), lambda i, t: (i, t, c // BRANCH_W))
    st2 = pl.BlockSpec((None, 2, BRANCH_W), lambda i, t: (i, 0, 0))
    return pl.pallas_call(
        functools.partial(_sconv_kernel, tl=tl, nl=nl),
        grid=(b, nl),
        in_specs=[col(C_BB), col(C_CB), col(C_XB), col(C_GB), st2, _lw((3, BRANCH_W), l)],
        out_specs=[pl.BlockSpec((None, tl, BRANCH_W), lambda i, t: (i, t, 0)), st2],
        out_shape=[jax.ShapeDtypeStruct((b, seq, BRANCH_W), BF16), jax.ShapeDtypeStruct((b, 2, BRANCH_W), F32)],
        scratch_shapes=[pltpu.VMEM((tl + 8, BRANCH_W), F32)],
        compiler_params=_params("parallel", "arbitrary"),
        name="sconv",
    )(p, p, p, p, cprev, w["conv_b_w"])


def _ssd_kernel(xc_ref, bcc_ref, dt_ref, zc_ref, cpx_ref, cpb_ref, st0_ref, cwx_ref, cbx_ref, cwb_ref, cbb_ref,
                dtb_ref, alog_ref, dexp_ref, nw_ref, e_ref,
                y_ref, cnx_ref, cnb_ref, st_ref, extx_s, extb_s, st_s, *, q, nc, nl):
    t = q * nc
    l = pl.program_id(1)

    @pl.when(l == 0)
    def _():
        extx_s[5:8, :] = cpx_ref[...]
        extb_s[5:8, :] = cpb_ref[...]
        st_s[...] = st0_ref[...]

    xs_all = _silu(_conv_tile(extx_s, xc_ref[...], cwx_ref, 4, t) + cbx_ref[...])
    tailx = _conv_carry(extx_s, 4, t)
    bcs_all = _silu(_conv_tile(extb_s, bcc_ref[...], cwb_ref, 4, t) + cbb_ref[...])
    tailb = _conv_carry(extb_s, 4, t)

    dt_all = _softplus(dt_ref[...] + dtb_ref[...])
    da_all = dt_all * (-jnp.exp(alog_ref[...]))
    gate_all = _silu(zc_ref[...])
    rows = lax.broadcasted_iota(jnp.int32, (q, q), 0)
    cols = lax.broadcasted_iota(jnp.int32, (q, q), 1)
    tri = rows >= cols
    eye = (lax.broadcasted_iota(jnp.int32, (LANE, LANE), 0) == lax.broadcasted_iota(jnp.int32, (LANE, LANE), 1))
    eye_b = eye.astype(BF16)
    expand = e_ref[...]

    def chunk(r0):
        xs, bcs = xs_all[r0:r0 + q, :], bcs_all[r0:r0 + q, :]
        dt, d_a = dt_all[r0:r0 + q, :], da_all[r0:r0 + q, :]
        acum = _sum3(lambda part: _bdot(tri.astype(BF16), part), d_a)
        acum_t = _sum3(lambda part: lax.dot_general(eye_b, part, NT_DIMS, preferred_element_type=F32), acum)
        dt_x = _sum3(lambda part: _bdot(part, expand), dt)
        acum_x = _sum3(lambda part: _bdot(part, expand), acum)
        end_x = acum_x[q - 1:q, :]
        xdt = xs * dt_x
        xw = (xdt * jnp.exp(end_x - acum_x)).astype(BF16)
        xdt_b = xdt.astype(BF16)
        eacum_x = jnp.exp(acum_x)
        cdecay_x = jnp.exp(end_x)

        y_parts = []
        for g in range(N_GROUPS_C):
            bg = bcs[:, g * D_STATE:(g + 1) * D_STATE].astype(BF16)
            cg = bcs[:, (N_GROUPS_C + g) * D_STATE:(N_GROUPS_C + g + 1) * D_STATE].astype(BF16)
            cb = lax.dot_general(cg, bg, NT_DIMS, preferred_element_type=F32)
            bg_t = lax.dot_general(eye_b, bg, NT_DIMS, preferred_element_type=F32).astype(BF16)
            for e in range(N_HEADS_C // N_GROUPS_C):
                h = g * (N_HEADS_C // N_GROUPS_C) + e
                lo = h * HEAD_DIM_C
                seg = acum_x[:, lo:lo + q] - acum_t[h:h + 1, :]
                decay = jnp.exp(jnp.where(tri, seg, -jnp.inf))
                y_diag = _bdot((cb * decay).astype(BF16), xdt_b[:, lo:lo + HEAD_DIM_C])
                prev = st_s[h]
                y_off = _bdot(cg, prev.astype(BF16)) * eacum_x[:, lo:lo + HEAD_DIM_C]
                st_s[h] = prev * cdecay_x[:, lo:lo + HEAD_DIM_C] + _bdot(bg_t, xw[:, lo:lo + HEAD_DIM_C])
                y_parts.append(y_diag + y_off)
        y = jnp.concatenate(y_parts, axis=1) + dexp_ref[...] * xs

        gated = y * gate_all[r0:r0 + q, :]
        half = BRANCH_W // N_GROUPS_C
        normed = jnp.concatenate(
            [gated[:, i * half:(i + 1) * half]
             * lax.rsqrt(jnp.mean(gated[:, i * half:(i + 1) * half] ** 2, axis=-1, keepdims=True) + EPS)
             for i in range(N_GROUPS_C)], axis=1)
        y_ref[r0:r0 + q, :] = (normed * nw_ref[...]).astype(y_ref.dtype)

    for ci in range(nc):
        chunk(ci * q)

    @pl.when(l == nl - 1)
    def _():
        cnx_ref[...] = tailx
        cnb_ref[...] = tailb
        st_ref[...] = st_s[...]


def _ssd(p, cprev_x, cprev_b, st0, w, l):
    b, seq, _ = p.shape
    q = CHUNK if seq % CHUNK == 0 else seq
    nc = 4 if (seq // q) % 4 == 0 else 1
    tt = q * nc
    nl = seq // tt
    st_shape = (N_HEADS_C, D_STATE, HEAD_DIM_C)
    colx = lambda c, wd: pl.BlockSpec((None, tt, wd), lambda i, t: (i, t, c // wd))
    st3 = lambda wd: pl.BlockSpec((None, 3, wd), lambda i, t: (i, 0, 0))
    st_spec = pl.BlockSpec((None,) + st_shape, lambda i, t: (i, 0, 0, 0))
    return pl.pallas_call(
        functools.partial(_ssd_kernel, q=q, nc=nc, nl=nl),
        grid=(b, nl),
        in_specs=[colx(C_XC, BRANCH_W), colx(C_BCC, BC_W), colx(C_DT, LANE), colx(C_ZC, BRANCH_W),
                  st3(BRANCH_W), st3(BC_W), st_spec,
                  _lw((4, BRANCH_W), l), _lw((1, BRANCH_W), l), _lw((4, BC_W), l), _lw((1, BC_W), l),
                  _lw((1, LANE), l), _lw((1, LANE), l), _lw((1, BRANCH_W), l), _lw((1, BRANCH_W), l),
                  pl.BlockSpec((LANE, BRANCH_W), lambda i, t: (0, 0))],
        out_specs=[pl.BlockSpec((None, tt, BRANCH_W), lambda i, t: (i, t, 0)), st3(BRANCH_W), st3(BC_W), st_spec],
        out_shape=[jax.ShapeDtypeStruct((b, seq, BRANCH_W), BF16), jax.ShapeDtypeStruct((b, 3, BRANCH_W), F32),
                   jax.ShapeDtypeStruct((b, 3, BC_W), F32), jax.ShapeDtypeStruct((b,) + st_shape, F32)],
        scratch_shapes=[pltpu.VMEM((tt + 8, BRANCH_W), F32), pltpu.VMEM((tt + 8, BC_W), F32), pltpu.VMEM(st_shape, F32)],
        compiler_params=_params("parallel", "arbitrary"),
        name="ssd",
    )(p, p, p, p, cprev_x, cprev_b, st0, w["conv_cx_w"], w["conv_cx_b"], w["conv_cb_w"], w["conv_cb_b"],
      w["dt_bias"], w["a_log"], w["d_exp"], w["ssd_norm_w"], w["expand"])


def _mla_prep_kernel(ql_ref, kvl_ref, kraw_ref, cq_ref, sq_ref, ck_ref, sk_ref, qnw_ref, kvnw_ref, wq_ref, wqs_ref,
                     q_ref, ckv_ref, kpe_ref, kpe128_ref):
    qn = _rms(ql_ref[...], qnw_ref[...]).astype(BF16)
    q = _bdot(qn, wq_ref[...])
    q_sw = _bdot(qn, wqs_ref[...])
    cos_q, sin_q = cq_ref[...], sq_ref[...]
    for h in range(N_HEADS_D):
        sl = slice(h * Q_SLAB, (h + 1) * Q_SLAB)
        q_ref[:, sl] = (q[:, sl] * cos_q + q_sw[:, sl] * sin_q).astype(q_ref.dtype)
    ckv_ref[...] = _rms(kvl_ref[...], kvnw_ref[...])
    kraw = kraw_ref[...]
    lane = lax.broadcasted_iota(jnp.int32, kraw.shape, 1)
    swapped = jnp.where(lane < HALF, pltpu.roll(kraw, LANE - HALF, axis=1), pltpu.roll(kraw, HALF, axis=1))
    kr = kraw * ck_ref[...] + swapped * sk_ref[...]
    kpe128_ref[...] = kr
    kpe_ref[...] = kr[:, :ROPE_DIM]


def _mla_prep(p2d, tabs, w, l):
    m = p2d.shape[0]
    tm = _pick(m, 256)
    col = lambda c, wd: pl.BlockSpec((tm, wd), lambda i: (i, c // wd))
    row = lambda wd: pl.BlockSpec((tm, wd), lambda i: (i, 0))
    qw = N_HEADS_D * Q_SLAB
    return pl.pallas_call(
        _mla_prep_kernel,
        grid=(m // tm,),
        in_specs=[col(C_QL, Q_LORA), col(C_KVL, KV_LORA), col(C_KR, LANE),
                  row(Q_SLAB), row(Q_SLAB), row(LANE), row(LANE),
                  _lw((1, Q_LORA), l), _lw((1, KV_LORA), l), _lw((Q_LORA, qw), l), _lw((Q_LORA, qw), l)],
        out_specs=[row(qw), row(KV_LORA), row(ROPE_DIM), row(LANE)],
        out_shape=[jax.ShapeDtypeStruct((m, qw), BF16), jax.ShapeDtypeStruct((m, KV_LORA), F32),
                   jax.ShapeDtypeStruct((m, ROPE_DIM), F32), jax.ShapeDtypeStruct((m, LANE), F32)],
        compiler_params=_params("parallel"),
        name="mla_prep",
    )(p2d, p2d, p2d, tabs["cos_q"], tabs["sin_q"], tabs["cos_k"], tabs["sin_k"],
      w["mla_q_norm_w"], w["mla_kv_norm_w"], w["wq"], w["wq_sw"])


def _kv_expand_kernel(ckv_ref, kpe_ref, wk_ref, wv_ref, k_ref, v_ref):
    c = ckv_ref[...].astype(BF16)
    kn = _bdot(c, wk_ref[...])
    v_ref[...] = _bdot(c, wv_ref[...]).astype(v_ref.dtype)
    kr = kpe_ref[...]
    for h in range(N_HEADS_D):
        k_ref[:, h * Q_SLAB:h * Q_SLAB + NOPE_DIM] = kn[:, h * NOPE_DIM:(h + 1) * NOPE_DIM].astype(k_ref.dtype)
        k_ref[:, h * Q_SLAB + NOPE_DIM:(h + 1) * Q_SLAB] = kr.astype(k_ref.dtype)


def _kv_expand(ckv, kpe128, w, l):
    r = ckv.shape[0]
    tr = _pick(r, 512)
    row = lambda wd: pl.BlockSpec((tr, wd), lambda i: (i, 0))
    kw, vw = N_HEADS_D * Q_SLAB, N_HEADS_D * V_DIM
    return pl.pallas_call(
        _kv_expand_kernel,
        grid=(r // tr,),
        in_specs=[row(KV_LORA), row(LANE), _lw((KV_LORA, N_HEADS_D * NOPE_DIM), l), _lw((KV_LORA, vw), l)],
        out_specs=[row(kw), row(vw)],
        out_shape=[jax.ShapeDtypeStruct((r, kw), BF16), jax.ShapeDtypeStruct((r, vw), BF16)],
        compiler_params=_params("parallel"),
        name="kv_expand",
    )(ckv, kpe128, w["wk"], w["wv"])


def _attn_kernel(qi_ref, kj_ref, fin_ref, q_ref, k_ref, v_ref, gd_ref, o_ref, m_s, l_s, acc_s, s_a, s_b, p_a, p_b,
                 *, tq, tk, strip, q_off, lk):
    step = pl.program_id(1)
    j = kj_ref[step]
    q_lo = q_off + qi_ref[step] * tq
    k_lo = j * tk
    all_visible = jnp.minimum((q_lo // CHUNK + 1) * CHUNK, lk)
    s_bufs, p_bufs = (s_a, s_b), (p_a, p_b)

    @pl.when(j == 0)
    def _():
        m_s[...] = jnp.full(m_s.shape, NEG_BIG, F32)
        l_s[...] = jnp.zeros(l_s.shape, F32)
        acc_s[...] = jnp.zeros(acc_s.shape, F32)

    def block(masked):
        if masked:
            qpos = q_lo + lax.broadcasted_iota(jnp.int32, (tq, tk), 0)
            kpos = k_lo + lax.broadcasted_iota(jnp.int32, (tq, tk), 1)
            visible = kpos < jnp.minimum((qpos // CHUNK + 1) * CHUNK, lk)
        def scores(h):
            q = q_ref[:, h * Q_SLAB:(h + 1) * Q_SLAB]
            k = k_ref[:, h * Q_SLAB:(h + 1) * Q_SLAB]
            s = lax.dot_general(q, k, NT_DIMS, preferred_element_type=F32)
            if masked:
                s = jnp.where(visible, s, NEG_BIG)
            s_bufs[h % 2][...] = s
            m_prev = m_s[h]
            m_next = jnp.maximum(m_prev, jnp.max(s, axis=1, keepdims=True))
            m_s[h] = m_next
            return m_next, jnp.exp2((m_prev - m_next) * EXP2_SCALE)

        def finish(h, m_next, alpha):
            s_buf, p_buf = s_bufs[h % 2], p_bufs[h % 2]
            for r in range(tq // strip):
                rows = slice(r * strip, (r + 1) * strip)
                pr = jnp.exp2((s_buf[rows, :] - m_next[rows, 0:1]) * EXP2_SCALE)
                part = pr[:, 0:LANE]
                for c in range(1, tk // LANE):
                    part = part + pr[:, c * LANE:(c + 1) * LANE]
                l_s[h, rows, :] = alpha[rows] * l_s[h, rows, :] + part
                p_buf[rows, :] = pr.astype(BF16)
            acc_s[h] = acc_s[h] * alpha + _bdot(p_buf[...], v_ref[:, h * V_DIM:(h + 1) * V_DIM])

        stats = scores(0)
        for h in range(N_HEADS_D):
            nxt = scores(h + 1) if h + 1 < N_HEADS_D else None
            finish(h, *stats)
            stats = nxt

    @pl.when(k_lo + tk <= all_visible)
    def _():
        block(False)

    @pl.when(k_lo + tk > all_visible)
    def _():
        block(True)

    @pl.when(fin_ref[step] == 1)
    def _():
        gate = _silu(gd_ref[...])
        for h in range(N_HEADS_D):
            sl = slice(h * V_DIM, (h + 1) * V_DIM)
            denom = jnp.sum(l_s[h], axis=1, keepdims=True)
            o_ref[:, sl] = (acc_s[h] / denom * gate[:, sl]).astype(o_ref.dtype)


def _attention(q, k, v, p, q_off, lk):
    b, lq, _ = q.shape
    lkp = k.shape[1]
    tq = _pick(lq, 512)
    tk = 1024 if lkp % 1024 == 0 else lkp
    strip = 16 if tq % 16 == 0 else SUBLANE
    nq = lq // tq
    qi, kj, fin = [], [], []
    for i in range(nq):
        last_bound = min(((q_off + (i + 1) * tq - 1) // CHUNK + 1) * CHUNK, lk)
        last_j = (last_bound - 1) // tk
        for j in range(last_j + 1):
            qi.append(i)
            kj.append(j)
            fin.append(int(j == last_j))
    to_arr = lambda a: jnp.asarray(np.asarray(a, np.int32))
    qw, vw = N_HEADS_D * Q_SLAB, N_HEADS_D * V_DIM
    grid_spec = pltpu.PrefetchScalarGridSpec(
        num_scalar_prefetch=3,
        grid=(b, len(qi)),
        in_specs=[pl.BlockSpec((None, tq, qw), lambda bi, s, qi_r, kj_r, fin_r: (bi, qi_r[s], 0)),
                  pl.BlockSpec((None, tk, qw), lambda bi, s, qi_r, kj_r, fin_r: (bi, kj_r[s], 0)),
                  pl.BlockSpec((None, tk, vw), lambda bi, s, qi_r, kj_r, fin_r: (bi, kj_r[s], 0)),
                  pl.BlockSpec((None, tq, vw), lambda bi, s, qi_r, kj_r, fin_r: (bi, qi_r[s], C_GD // vw))],
        out_specs=pl.BlockSpec((None, tq, vw), lambda bi, s, qi_r, kj_r, fin_r: (bi, qi_r[s], 0)),
        scratch_shapes=[pltpu.VMEM((N_HEADS_D, tq, LANE), F32), pltpu.VMEM((N_HEADS_D, tq, LANE), F32),
                        pltpu.VMEM((N_HEADS_D, tq, V_DIM), F32), pltpu.VMEM((tq, tk), F32), pltpu.VMEM((tq, tk), F32),
                        pltpu.VMEM((tq, tk), BF16), pltpu.VMEM((tq, tk), BF16)],
    )
    return pl.pallas_call(
        functools.partial(_attn_kernel, tq=tq, tk=tk, strip=strip, q_off=q_off, lk=lk),
        grid_spec=grid_spec,
        out_shape=jax.ShapeDtypeStruct((b, lq, vw), BF16),
        compiler_params=_params("parallel", "arbitrary"),
        name="mla_attention",
    )(to_arr(qi), to_arr(kj), to_arr(fin), q, k, v, p)


def _prep_weights(pre_norm_w, w_in, conv_a_w, conv_a_b, lru_w_r, lru_b_r, lru_w_i, lru_b_i, lru_lambda, conv_b_w,
                  conv_c_w, conv_c_b, ssd_dt_bias, ssd_a_log, ssd_d, ssd_norm_w, mla_q_norm_w, mla_w_q_up,
                  mla_kv_norm_w, mla_w_kv_up, w_branch_out, w_out, post_norm_w):
    d = w_in.shape[0]
    w_all = _pack_w_in(w_in)

    wq = mla_w_q_up.reshape(d, Q_LORA, N_HEADS_D, NOPE_DIM + ROPE_DIM)
    nope, x1, x2 = wq[..., :NOPE_DIM], wq[..., NOPE_DIM:NOPE_DIM + HALF], wq[..., NOPE_DIM + HALF:]
    zq = lambda n: jnp.zeros((d, Q_LORA, N_HEADS_D, n), wq.dtype)
    wq_main = jnp.concatenate([nope, x1, x2, zq(Q_SLAB - NOPE_DIM - ROPE_DIM)], -1).reshape(d, Q_LORA, -1).astype(BF16)
    wq_sw = jnp.concatenate([zq(NOPE_DIM), -x2, x1, zq(Q_SLAB - NOPE_DIM - ROPE_DIM)], -1).reshape(d, Q_LORA, -1).astype(BF16)
    wkv = mla_w_kv_up.reshape(d, KV_LORA, N_HEADS_D, NOPE_DIM + V_DIM)
    wk = wkv[..., :NOPE_DIM].reshape(d, KV_LORA, -1).astype(BF16)
    wv = wkv[..., NOPE_DIM:].reshape(d, KV_LORA, -1).astype(BF16)

    pad_heads = lambda a: jnp.pad(a, ((0, 0), (0, LANE - N_HEADS_C)))[:, None, :]
    expand = (jnp.arange(LANE)[:, None] == (jnp.arange(BRANCH_W) // HEAD_DIM_C)[None, :]).astype(BF16)
    vec = lambda a: a[:, None, :]
    return dict(
        pre_norm_w=vec(pre_norm_w), post_norm_w=vec(post_norm_w), w_all=w_all,
        w_branch=w_branch_out.astype(BF16), w_out=w_out.astype(BF16),
        conv_a_w=conv_a_w, conv_a_b=vec(conv_a_b), lru_w_r=lru_w_r.astype(BF16), lru_b_r=vec(lru_b_r),
        lru_w_i=lru_w_i.astype(BF16), lru_b_i=vec(lru_b_i), lru_lambda=vec(lru_lambda), conv_b_w=conv_b_w,
        conv_cx_w=conv_c_w[..., :BRANCH_W], conv_cx_b=vec(conv_c_b[..., :BRANCH_W]),
        conv_cb_w=conv_c_w[..., BRANCH_W:], conv_cb_b=vec(conv_c_b[..., BRANCH_W:]),
        dt_bias=pad_heads(ssd_dt_bias), a_log=pad_heads(ssd_a_log), d_exp=vec(jnp.repeat(ssd_d, HEAD_DIM_C, axis=-1)),
        ssd_norm_w=vec(ssd_norm_w), expand=expand,
        mla_q_norm_w=vec(mla_q_norm_w), mla_kv_norm_w=vec(mla_kv_norm_w), wq=wq_main, wq_sw=wq_sw, wk=wk, wv=wv,
    )


def _rope_tables(pos, reps):
    inv = ROPE_THETA ** (-jnp.arange(HALF, dtype=F32) / HALF)
    ang = pos.astype(F32)[:, None] * inv[None, :]
    cos, sin = jnp.cos(ang), jnp.sin(ang)
    n = pos.shape[0]
    z = lambda w: jnp.zeros((n, w), F32)
    tabs = dict(
        cos_q=jnp.concatenate([jnp.ones((n, NOPE_DIM), F32), cos, cos, z(Q_SLAB - NOPE_DIM - ROPE_DIM)], 1),
        sin_q=jnp.concatenate([z(NOPE_DIM), sin, sin, z(Q_SLAB - NOPE_DIM - ROPE_DIM)], 1),
        cos_k=jnp.concatenate([cos, cos, z(LANE - ROPE_DIM)], 1),
        sin_k=jnp.concatenate([-sin, sin, z(LANE - ROPE_DIM)], 1),
    )
    return {k: jnp.tile(v, (reps, 1)) for k, v in tabs.items()}


def _layer(x, h, b, seq, tabs, q_off, state, past, w, l, last):
    conv_a_prev, h0, conv_b_prev, conv_c_prev, ssd0 = state
    m = b * seq
    p2d = _matmul(h, w["w_all"], l, PACK_W, 1024, F32, "in_proj")
    p = p2d.reshape(b, seq, PACK_W)

    y_a, h_last, conv_a_new = _lru(p, conv_a_prev, h0[:, None, :], w, l)
    y_b, conv_b_new = _sconv(p, conv_b_prev, w, l)
    y_c, conv_cx_new, conv_cb_new, ssd_t = _ssd(p, conv_c_prev[..., :BRANCH_W], conv_c_prev[..., BRANCH_W:],
                                                 jnp.swapaxes(ssd0, -1, -2), w, l)
    q, ckv, kpe, kpe128 = _mla_prep(p2d, tabs, w, l)
    if past is None:
        ckv_all, kpe_all, lk, lkp = ckv, kpe128, seq, seq
    else:
        past_ckv, past_kpe = past
        lk = past_ckv.shape[1] + seq
        lkp = -(-lk // LANE) * LANE
        tail = lambda wd: jnp.zeros((b, lkp - lk, wd), F32)
        ckv_all = jnp.concatenate([past_ckv, ckv.reshape(b, seq, KV_LORA), tail(KV_LORA)], axis=1).reshape(b * lkp, KV_LORA)
        past_kpe128 = jnp.pad(past_kpe, ((0, 0), (0, 0), (0, LANE - ROPE_DIM)))
        kpe_all = jnp.concatenate([past_kpe128, kpe128.reshape(b, seq, LANE), tail(LANE)], axis=1).reshape(b * lkp, LANE)
    k, v = _kv_expand(ckv_all, kpe_all, w, l)
    y_d = _attention(q.reshape(b, seq, -1), k.reshape(b, lkp, -1), v.reshape(b, lkp, -1), p, q_off, lk)

    merged = _merge(h, [y.reshape(m, BRANCH_W) for y in (y_a, y_b, y_c, y_d)], w["w_all"], w["w_branch"], l)
    out = _matmul(merged, w["w_out"], l, D_MODEL, 1024, F32, "out_proj")
    x_new, h_next = _post(x, out, w["post_norm_w"], w["pre_norm_w"], l, last)
    new_state = (ckv.reshape(b, seq, KV_LORA), kpe.reshape(b, seq, ROPE_DIM), conv_a_new, h_last[:, 0, :], conv_b_new,
                 jnp.concatenate([conv_cx_new, conv_cb_new], axis=-1), jnp.swapaxes(ssd_t, -1, -2))
    return x_new, h_next, new_state


def kernel(x_prompt, x_sample, cache_mla_latent, cache_mla_kpe, state_lru_conv, state_lru_h, state_sconv, state_ssd_conv, state_ssd, pre_norm_w, w_in, conv_a_w, conv_a_b, lru_w_r, lru_b_r, lru_w_i, lru_b_i, lru_lambda, conv_b_w, conv_c_w, conv_c_b, ssd_dt_bias, ssd_a_log, ssd_d, ssd_norm_w, mla_q_norm_w, mla_w_q_up, mla_kv_norm_w, mla_w_kv_up, w_branch_out, w_out, post_norm_w):
    bp, sp, d = x_prompt.shape
    bs, ss, _ = x_sample.shape
    past = cache_mla_latent.shape[2]
    depth = w_in.shape[0]
    w = _prep_weights(pre_norm_w, w_in, conv_a_w, conv_a_b, lru_w_r, lru_b_r, lru_w_i, lru_b_i, lru_lambda, conv_b_w,
                      conv_c_w, conv_c_b, ssd_dt_bias, ssd_a_log, ssd_d, ssd_norm_w, mla_q_norm_w, mla_w_q_up,
                      mla_kv_norm_w, mla_w_kv_up, w_branch_out, w_out, post_norm_w)
    tabs_p = _rope_tables(jnp.arange(sp), bp)
    tabs_s = _rope_tables(past + jnp.arange(ss), bs)
    zeros_p = (jnp.zeros((bp, 3, BRANCH_W), F32), jnp.zeros((bp, BRANCH_W), F32), jnp.zeros((bp, 2, BRANCH_W), F32),
               jnp.zeros((bp, 3, BRANCH_W + BC_W), F32), jnp.zeros((bp, N_HEADS_C, HEAD_DIM_C, D_STATE), F32))

    xp = x_prompt.reshape(bp * sp, d)
    xs = x_sample.reshape(bs * ss, d)
    hp = _rms_cast(xp, w["pre_norm_w"], 0)
    hs = _rms_cast(xs, w["pre_norm_w"], 0)
    p_st = [[] for _ in range(7)]
    s_st = [[] for _ in range(7)]
    for l in range(depth):
        last = l == depth - 1
        xp, hp, new_p = _layer(xp, hp, bp, sp, tabs_p, 0, zeros_p, None, w, l, last)
        st_s = (state_lru_conv[l], state_lru_h[l], state_sconv[l], state_ssd_conv[l], state_ssd[l])
        xs, hs, new_s = _layer(xs, hs, bs, ss, tabs_s, past, st_s, (cache_mla_latent[l], cache_mla_kpe[l]), w, l, last)
        for i in range(7):
            p_st[i].append(new_p[i])
            s_st[i].append(new_s[i])
    p_out = [jnp.stack(v, axis=0) for v in p_st]
    s_out = [jnp.stack(v, axis=0) for v in s_st]
    return (xp.reshape(bp, sp, d), xs.reshape(bs, ss, d), *p_out, *s_out)
```

```python
import functools

import numpy as np
import jax
import jax.numpy as jnp
from jax import lax
from jax.experimental import pallas as pl
from jax.experimental.pallas import tpu as pltpu

F32 = jnp.float32
BF16 = jnp.bfloat16

D_MODEL = 4096
CHUNK = 64
BRANCH_W = 1024
N_BRANCH = 4
EPS = 1e-6
LRU_BLOCKS = 8
LRU_BLOCK = 128
LRU_C = 8.0
N_HEADS_C = 16
HEAD_DIM_C = 64
N_GROUPS_C = 2
D_STATE = 128
N_HEADS_D = 8
NOPE_DIM = 128
ROPE_DIM = 64
V_DIM = 128
Q_LORA = 1024
KV_LORA = 512
ROPE_THETA = 10000.0
ATTN_SCALE = (NOPE_DIM + ROPE_DIM) ** -0.5
HALF = ROPE_DIM // 2
BC_W = 2 * N_GROUPS_C * D_STATE

LANE = 128
SUBLANE = 8
VMEM_LIMIT = 56 * 1024 * 1024

O_BC = 7168
O_DT = 7680
O_ZC = 7696
O_QL = 8720
O_KVL = 9744
O_KPE = 10256
O_GD = 10320
O_MG = 11344
PACK_TILE = 512
PACK_SRC = (tuple(range(0, O_DT, PACK_TILE)) + (O_DT,) + (O_ZC, O_ZC + PACK_TILE) + (O_QL, O_QL + PACK_TILE)
            + (O_KVL, O_KPE) + (O_GD, O_GD + PACK_TILE))
PACK_W = len(PACK_SRC) * PACK_TILE
C_XA, C_GA, C_BB, C_CB, C_XB, C_GB, C_XC = (i * BRANCH_W for i in range(7))
C_BCC = 7168
C_DT = 7680
C_ZC = 8192
C_QL = 9216
C_KVL = 10240
C_KR = 10752
C_GD = 11264
Q_SLAB = 256

NT_DIMS = (((1,), (1,)), ((), ()))
HI = lax.Precision.HIGHEST
EXP2_SCALE = ATTN_SCALE * 1.4426950408889634
NEG_BIG = -1e30


def _pick(n, pref):
    if n <= pref:
        return n
    t = pref
    while n % t:
        t -= SUBLANE
    return t


def _params(*sem):
    return pltpu.CompilerParams(dimension_semantics=sem, vmem_limit_bytes=VMEM_LIMIT)


def _lw(shape, l):
    idx = (l,) + (0,) * len(shape)
    return pl.BlockSpec((None,) + tuple(shape), lambda *_: idx)


def _sigmoid(x):
    return 1.0 / (1.0 + jnp.exp(-x))


def _silu(x):
    return x * _sigmoid(x)


def _softplus(x):
    return jnp.maximum(x, 0.0) + jnp.log1p(jnp.exp(-jnp.abs(x)))


def _neg_expm1(y):
    e = jnp.exp(y)
    near = jnp.where(e == 1.0, -y, (1.0 - e) * y / jnp.log(e))
    return jnp.where(y > -0.5, near, 1.0 - e)


def _rms(x, w):
    return x * lax.rsqrt(jnp.mean(x * x, axis=-1, keepdims=True) + EPS) * w


def _bdot(a, b):
    return jnp.dot(a, b, preferred_element_type=F32)


def _sum3(dot_with, x):
    x1 = x.astype(BF16)
    r1 = x - x1.astype(F32)
    x2 = r1.astype(BF16)
    x3 = (r1 - x2.astype(F32)).astype(BF16)
    return dot_with(x1) + dot_with(x2) + dot_with(x3)


def _pack_kernel(off_ref, a_ref, o_ref):
    del off_ref
    o_ref[...] = a_ref[0].T.astype(o_ref.dtype)


def _pack_w_in(w_in):
    d, k, n = w_in.shape
    wt = jnp.swapaxes(w_in, 1, 2)
    src = PACK_SRC + tuple(range(O_MG, n, PACK_TILE))
    grid_spec = pltpu.PrefetchScalarGridSpec(
        num_scalar_prefetch=1,
        grid=(d, len(src)),
        in_specs=[pl.BlockSpec((pl.Element(1), pl.Element(PACK_TILE), pl.Element(k)), lambda l, j, off: (l, pl.multiple_of(off[j], 16), 0))],
        out_specs=pl.BlockSpec((None, k, PACK_TILE), lambda l, j, off: (l, 0, j)),
    )
    return pl.pallas_call(
        _pack_kernel, grid_spec=grid_spec, out_shape=jax.ShapeDtypeStruct((d, k, len(src) * PACK_TILE), BF16),
        compiler_params=_params("parallel", "parallel"), name="pack_w_in",
    )(jnp.asarray(np.asarray(src, np.int32)), wt)


def _rms_cast_kernel(x_ref, w_ref, o_ref):
    o_ref[...] = _rms(x_ref[...], w_ref[...]).astype(o_ref.dtype)


def _rms_cast(x, w, l):
    m, d = x.shape
    tm = _pick(m, 256)
    return pl.pallas_call(
        _rms_cast_kernel,
        grid=(m // tm,),
        in_specs=[pl.BlockSpec((tm, d), lambda i: (i, 0)), _lw((1, d), l)],
        out_specs=pl.BlockSpec((tm, d), lambda i: (i, 0)),
        out_shape=jax.ShapeDtypeStruct((m, d), BF16),
        compiler_params=_params("parallel"),
        name="rms_cast",
    )(x, w)


def _mm_kernel(a_ref, w_ref, o_ref):
    o_ref[...] = _bdot(a_ref[...], w_ref[...]).astype(o_ref.dtype)


def _matmul(a, w, l, ncols, tn, out_dtype, name):
    m, k = a.shape
    tm = _pick(m, 1024)
    return pl.pallas_call(
        _mm_kernel,
        grid=(m // tm, ncols // tn),
        in_specs=[pl.BlockSpec((tm, k), lambda i, j: (i, 0)), pl.BlockSpec((None, k, tn), lambda i, j: (l, 0, j))],
        out_specs=pl.BlockSpec((tm, tn), lambda i, j: (i, j)),
        out_shape=jax.ShapeDtypeStruct((m, ncols), out_dtype),
        compiler_params=_params("parallel", "arbitrary"),
        name=name,
    )(a, w)


def _merge_kernel(h_ref, ya_ref, yb_ref, yc_ref, yd_ref, g0, g1, g2, g3, b0, b1, b2, b3, o_ref):
    h = h_ref[...]
    acc = None
    for y_ref, g_ref, b_ref in ((ya_ref, g0, b0), (yb_ref, g1, b1), (yc_ref, g2, b2), (yd_ref, g3, b3)):
        term = _sigmoid(_bdot(h, g_ref[...])) * _bdot(y_ref[...], b_ref[...])
        acc = term if acc is None else acc + term
    o_ref[...] = acc.astype(o_ref.dtype)


def _merge(h, ys, w_all, wb, l):
    m = h.shape[0]
    tm = _pick(m, 1024)
    tn = 256
    once = pl.Buffered(1)
    y_spec = pl.BlockSpec((tm, BRANCH_W), lambda i, j: (i, 0), pipeline_mode=once)
    gate0 = PACK_W // tn
    g_specs = [pl.BlockSpec((None, D_MODEL, tn), functools.partial(lambda i, j, n: (l, 0, gate0 + n * (D_MODEL // tn) + j), n=n))
               for n in range(N_BRANCH)]
    b_specs = [pl.BlockSpec((None, None, BRANCH_W, tn), functools.partial(lambda i, j, n: (l, n, 0, j), n=n)) for n in range(N_BRANCH)]
    return pl.pallas_call(
        _merge_kernel,
        grid=(m // tm, D_MODEL // tn),
        in_specs=[pl.BlockSpec((tm, D_MODEL), lambda i, j: (i, 0), pipeline_mode=once)] + [y_spec] * 4 + g_specs + b_specs,
        out_specs=pl.BlockSpec((tm, tn), lambda i, j: (i, j)),
        out_shape=jax.ShapeDtypeStruct((m, D_MODEL), BF16),
        compiler_params=_params("parallel", "arbitrary"),
        name="gate_merge",
    )(h, *ys, w_all, w_all, w_all, w_all, wb, wb, wb, wb)


def _post_kernel(x_ref, o_ref, pw_ref, nw_ref, xn_ref, hn_ref):
    xn = x_ref[...] + _rms(o_ref[...], pw_ref[...])
    xn_ref[...] = xn
    hn_ref[...] = _rms(xn, nw_ref[...]).astype(hn_ref.dtype)


def _post_last_kernel(x_ref, o_ref, pw_ref, xn_ref):
    xn_ref[...] = x_ref[...] + _rms(o_ref[...], pw_ref[...])


def _post(x, out, post_w, pre_w, l, last):
    m, d = x.shape
    tm = _pick(m, 256)
    row = pl.BlockSpec((tm, d), lambda i: (i, 0))
    if last:
        return pl.pallas_call(
            _post_last_kernel, grid=(m // tm,), in_specs=[row, row, _lw((1, d), l)], out_specs=row,
            out_shape=jax.ShapeDtypeStruct((m, d), F32), compiler_params=_params("parallel"), name="post_last",
        )(x, out, post_w), None
    return pl.pallas_call(
        _post_kernel, grid=(m // tm,), in_specs=[row, row, _lw((1, d), l), _lw((1, d), l + 1)], out_specs=[row, row],
        out_shape=[jax.ShapeDtypeStruct((m, d), F32), jax.ShapeDtypeStruct((m, d), BF16)],
        compiler_params=_params("parallel"), name="post",
    )(x, out, post_w, pre_w)


def _conv_tile(ext_s, x, w_ref, width, tl):
    ext_s[8:8 + tl, :] = x
    k0 = 9 - width
    y = ext_s[k0:k0 + tl, :] * w_ref[0:1, :]
    for j in range(1, width - 1):
        y = y + ext_s[k0 + j:k0 + j + tl, :] * w_ref[j:j + 1, :]
    return y + x * w_ref[width - 1:width, :]


def _conv_carry(ext_s, width, tl):
    tail = ext_s[tl + 9 - width:tl + 8, :]
    ext_s[9 - width:8, :] = tail
    return tail


def _lru_kernel(xa_ref, ga_ref, cprev_ref, h0_ref, cw_ref, cb_ref, wr_ref, br_ref, wi_ref, bi_ref, lam_ref,
                y_ref, hlast_ref, cnew_ref, ext_s, a_s, u_s, h_s, *, tl, nl):
    l = pl.program_id(1)

    @pl.when(l == 0)
    def _():
        ext_s[5:8, :] = cprev_ref[...]
        h_s[...] = h0_ref[...]

    xc = _conv_tile(ext_s, xa_ref[...], cw_ref, 4, tl) + cb_ref[...]
    tail = _conv_carry(ext_s, 4, tl)

    xcb = xc.astype(BF16)
    rs, gs = [], []
    for n in range(LRU_BLOCKS):
        blk = xcb[:, n * LRU_BLOCK:(n + 1) * LRU_BLOCK]
        rs.append(_bdot(blk, wr_ref[n]))
        gs.append(_bdot(blk, wi_ref[n]))
    r = _sigmoid(jnp.concatenate(rs, axis=1) + br_ref[...])
    gate_i = _sigmoid(jnp.concatenate(gs, axis=1) + bi_ref[...])
    log_a = (-LRU_C * r) * _softplus(-lam_ref[...])
    a_s[...] = jnp.exp(log_a)
    u_s[...] = jnp.sqrt(_neg_expm1(2.0 * log_a)) * (gate_i * xc)

    def step(t, h):
        h = a_s[pl.ds(t, 1), :] * h + u_s[pl.ds(t, 1), :]
        u_s[pl.ds(t, 1), :] = h
        return h

    h = lax.fori_loop(0, tl, step, h_s[...], unroll=8)
    h_s[...] = h
    y_ref[...] = (u_s[...] * _silu(ga_ref[...])).astype(y_ref.dtype)

    @pl.when(l == nl - 1)
    def _():
        hlast_ref[...] = h
        cnew_ref[...] = tail


def _lru(p, cprev, h0, w, l):
    b, seq, _ = p.shape
    tl = _pick(seq, 256)
    nl = seq // tl
    col = lambda c: pl.BlockSpec((None, tl, BRANCH_W), lambda i, t: (i, t, c // BRANCH_W))
    st3 = pl.BlockSpec((None, 3, BRANCH_W), lambda i, t: (i, 0, 0))
    st1 = pl.BlockSpec((None, 1, BRANCH_W), lambda i, t: (i, 0, 0))
    vec = _lw((1, BRANCH_W), l)
    blk = _lw((LRU_BLOCKS, LRU_BLOCK, LRU_BLOCK), l)
    return pl.pallas_call(
        functools.partial(_lru_kernel, tl=tl, nl=nl),
        grid=(b, nl),
        in_specs=[col(C_XA), col(C_GA), st3, st1, _lw((4, BRANCH_W), l), vec, blk, vec, blk, vec, vec],
        out_specs=[pl.BlockSpec((None, tl, BRANCH_W), lambda i, t: (i, t, 0)), st1, st3],
        out_shape=[jax.ShapeDtypeStruct((b, seq, BRANCH_W), BF16), jax.ShapeDtypeStruct((b, 1, BRANCH_W), F32),
                   jax.ShapeDtypeStruct((b, 3, BRANCH_W), F32)],
        scratch_shapes=[pltpu.VMEM((tl + 8, BRANCH_W), F32), pltpu.VMEM((tl, BRANCH_W), F32),
                        pltpu.VMEM((tl, BRANCH_W), F32), pltpu.VMEM((1, BRANCH_W), F32)],
        compiler_params=_params("parallel", "arbitrary"),
        name="rglru",
    )(p, p, cprev, h0, w["conv_a_w"], w["conv_a_b"], w["lru_w_r"], w["lru_b_r"], w["lru_w_i"], w["lru_b_i"], w["lru_lambda"])


def _sconv_kernel(bb_ref, cb_ref, xb_ref, gb_ref, cprev_ref, cw_ref, y_ref, cnew_ref, ext_s, *, tl, nl):
    l = pl.program_id(1)

    @pl.when(l == 0)
    def _():
        ext_s[6:8, :] = cprev_ref[...]

    v = _conv_tile(ext_s, cb_ref[...] * xb_ref[...], cw_ref, 3, tl)
    tail = _conv_carry(ext_s, 3, tl)
    y_ref[...] = (bb_ref[...] * v * _silu(gb_ref[...])).astype(y_ref.dtype)

    @pl.when(l == nl - 1)
    def _():
        cnew_ref[...] = tail


def _sconv(p, cprev, w, l):
    b, seq, _ = p.shape
    tl = _pick(seq, 256)
    nl = seq // tl
    col = lambda c: pl.BlockSpec((None, tl, BRANCH_W), lambda i, t: (i, t, c // BRANCH_W))
    st2 = pl.BlockSpec((None, 2, BRANCH_W), lambda i, t: (i, 0, 0))
    return pl.pallas_call(
        functools.partial(_sconv_kernel, tl=tl, nl=nl),
        grid=(b, nl),
        in_specs=[col(C_BB), col(C_CB), col(C_XB), col(C_GB), st2, _lw((3, BRANCH_W), l)],
        out_specs=[pl.BlockSpec((None, tl, BRANCH_W), lambda i, t: (i, t, 0)), st2],
        out_shape=[jax.ShapeDtypeStruct((b, seq, BRANCH_W), BF16), jax.ShapeDtypeStruct((b, 2, BRANCH_W), F32)],
        scratch_shapes=[pltpu.VMEM((tl + 8, BRANCH_W), F32)],
        compiler_params=_params("parallel", "arbitrary"),
        name="sconv",
    )(p, p, p, p, cprev, w["conv_b_w"])


def _ssd_kernel(xc_ref, bcc_ref, dt_ref, zc_ref, cpx_ref, cpb_ref, st0_ref, cwx_ref, cbx_ref, cwb_ref, cbb_ref,
                dtb_ref, alog_ref, dexp_ref, nw_ref, e_ref,
                y_ref, cnx_ref, cnb_ref, st_ref, extx_s, extb_s, st_s, *, q, nc, nl):
    t = q * nc
    l = pl.program_id(1)

    @pl.when(l == 0)
    def _():
        extx_s[5:8, :] = cpx_ref[...]
        extb_s[5:8, :] = cpb_ref[...]
        st_s[...] = st0_ref[...]

    xs_all = _silu(_conv_tile(extx_s, xc_ref[...], cwx_ref, 4, t) + cbx_ref[...])
    tailx = _conv_carry(extx_s, 4, t)
    bcs_all = _silu(_conv_tile(extb_s, bcc_ref[...], cwb_ref, 4, t) + cbb_ref[...])
    tailb = _conv_carry(extb_s, 4, t)

    dt_all = _softplus(dt_ref[...] + dtb_ref[...])
    da_all = dt_all * (-jnp.exp(alog_ref[...]))
    gate_all = _silu(zc_ref[...])
    rows = lax.broadcasted_iota(jnp.int32, (q, q), 0)
    cols = lax.broadcasted_iota(jnp.int32, (q, q), 1)
    tri = rows >= cols
    eye = (lax.broadcasted_iota(jnp.int32, (LANE, LANE), 0) == lax.broadcasted_iota(jnp.int32, (LANE, LANE), 1))
    eye_b = eye.astype(BF16)
    expand = e_ref[...]

    def chunk(r0):
        xs, bcs = xs_all[r0:r0 + q, :], bcs_all[r0:r0 + q, :]
        dt, d_a = dt_all[r0:r0 + q, :], da_all[r0:r0 + q, :]
        acum = _sum3(lambda part: _bdot(tri.astype(BF16), part), d_a)
        acum_t = _sum3(lambda part: lax.dot_general(eye_b, part, NT_DIMS, preferred_element_type=F32), acum)
        dt_x = _sum3(lambda part: _bdot(part, expand), dt)
        acum_x = _sum3(lambda part: _bdot(part, expand), acum)
        end_x = acum_x[q - 1:q, :]
        xdt = xs * dt_x
        xw = (xdt * jnp.exp(end_x - acum_x)).astype(BF16)
        xdt_b = xdt.astype(BF16)
        eacum_x = jnp.exp(acum_x)
        cdecay_x = jnp.exp(end_x)

        y_parts = []
        for g in range(N_GROUPS_C):
            bg = bcs[:, g * D_STATE:(g + 1) * D_STATE].astype(BF16)
            cg = bcs[:, (N_GROUPS_C + g) * D_STATE:(N_GROUPS_C + g + 1) * D_STATE].astype(BF16)
            cb = lax.dot_general(cg, bg, NT_DIMS, preferred_element_type=F32)
            bg_t = lax.dot_general(eye_b, bg, NT_DIMS, preferred_element_type=F32).astype(BF16)
            for e in range(N_HEADS_C // N_GROUPS_C):
                h = g * (N_HEADS_C // N_GROUPS_C) + e
                lo = h * HEAD_DIM_C
                seg = acum_x[:, lo:lo + q] - acum_t[h:h + 1, :]
                decay = jnp.exp(jnp.where(tri, seg, -jnp.inf))
                y_diag = _bdot((cb * decay).astype(BF16), xdt_b[:, lo:lo + HEAD_DIM_C])
                prev = st_s[h]
                y_off = _bdot(cg, prev.astype(BF16)) * eacum_x[:, lo:lo + HEAD_DIM_C]
                st_s[h] = prev * cdecay_x[:, lo:lo + HEAD_DIM_C] + _bdot(bg_t, xw[:, lo:lo + HEAD_DIM_C])
                y_parts.append(y_diag + y_off)
        y = jnp.concatenate(y_parts, axis=1) + dexp_ref[...] * xs

        gated = y * gate_all[r0:r0 + q, :]
        half = BRANCH_W // N_GROUPS_C
        normed = jnp.concatenate(
            [gated[:, i * half:(i + 1) * half]
             * lax.rsqrt(jnp.mean(gated[:, i * half:(i + 1) * half] ** 2, axis=-1, keepdims=True) + EPS)
             for i in range(N_GROUPS_C)], axis=1)
        y_ref[r0:r0 + q, :] = (normed * nw_ref[...]).astype(y_ref.dtype)

    for ci in range(nc):
        chunk(ci * q)

    @pl.when(l == nl - 1)
    def _():
        cnx_ref[...] = tailx
        cnb_ref[...] = tailb
        st_ref[...] = st_s[...]


def _ssd(p, cprev_x, cprev_b, st0, w, l):
    b, seq, _ = p.shape
    q = CHUNK if seq % CHUNK == 0 else seq
    nc = 4 if (seq // q) % 4 == 0 else 1
    tt = q * nc
    nl = seq // tt
    st_shape = (N_HEADS_C, D_STATE, HEAD_DIM_C)
    colx = lambda c, wd: pl.BlockSpec((None, tt, wd), lambda i, t: (i, t, c // wd))
    st3 = lambda wd: pl.BlockSpec((None, 3, wd), lambda i, t: (i, 0, 0))
    st_spec = pl.BlockSpec((None,) + st_shape, lambda i, t: (i, 0, 0, 0))
    return pl.pallas_call(
        functools.partial(_ssd_kernel, q=q, nc=nc, nl=nl),
        grid=(b, nl),
        in_specs=[colx(C_XC, BRANCH_W), colx(C_BCC, BC_W), colx(C_DT, LANE), colx(C_ZC, BRANCH_W),
                  st3(BRANCH_W), st3(BC_W), st_spec,
                  _lw((4, BRANCH_W), l), _lw((1, BRANCH_W), l), _lw((4, BC_W), l), _lw((1, BC_W), l),
                  _lw((1, LANE), l), _lw((1, LANE), l), _lw((1, BRANCH_W), l), _lw((1, BRANCH_W), l),
                  pl.BlockSpec((LANE, BRANCH_W), lambda i, t: (0, 0))],
        out_specs=[pl.BlockSpec((None, tt, BRANCH_W), lambda i, t: (i, t, 0)), st3(BRANCH_W), st3(BC_W), st_spec],
        out_shape=[jax.ShapeDtypeStruct((b, seq, BRANCH_W), BF16), jax.ShapeDtypeStruct((b, 3, BRANCH_W), F32),
                   jax.ShapeDtypeStruct((b, 3, BC_W), F32), jax.ShapeDtypeStruct((b,) + st_shape, F32)],
        scratch_shapes=[pltpu.VMEM((tt + 8, BRANCH_W), F32), pltpu.VMEM((tt + 8, BC_W), F32), pltpu.VMEM(st_shape, F32)],
        compiler_params=_params("parallel", "arbitrary"),
        name="ssd",
    )(p, p, p, p, cprev_x, cprev_b, st0, w["conv_cx_w"], w["conv_cx_b"], w["conv_cb_w"], w["conv_cb_b"],
      w["dt_bias"], w["a_log"], w["d_exp"], w["ssd_norm_w"], w["expand"])


def _mla_prep_kernel(ql_ref, kvl_ref, kraw_ref, cq_ref, sq_ref, ck_ref, sk_ref, qnw_ref, kvnw_ref, wq_ref, wqs_ref,
                     q_ref, ckv_ref, kpe_ref, kpe128_ref):
    qn = _rms(ql_ref[...], qnw_ref[...]).astype(BF16)
    q = _bdot(qn, wq_ref[...])
    q_sw = _bdot(qn, wqs_ref[...])
    cos_q, sin_q = cq_ref[...], sq_ref[...]
    for h in range(N_HEADS_D):
        sl = slice(h * Q_SLAB, (h + 1) * Q_SLAB)
        q_ref[:, sl] = ((q[:, sl] * cos_q + q_sw[:, sl] * sin_q) * EXP2_SCALE).astype(q_ref.dtype)
    ckv_ref[...] = _rms(kvl_ref[...], kvnw_ref[...])
    kraw = kraw_ref[...]
    lane = lax.broadcasted_iota(jnp.int32, kraw.shape, 1)
    swapped = jnp.where(lane < HALF, pltpu.roll(kraw, LANE - HALF, axis=1), pltpu.roll(kraw, HALF, axis=1))
    kr = kraw * ck_ref[...] + swapped * sk_ref[...]
    kpe128_ref[...] = kr
    kpe_ref[...] = kr[:, :ROPE_DIM]


def _mla_prep(p2d, tabs, w, l):
    m = p2d.shape[0]
    tm = _pick(m, 256)
    col = lambda c, wd: pl.BlockSpec((tm, wd), lambda i: (i, c // wd))
    row = lambda wd: pl.BlockSpec((tm, wd), lambda i: (i, 0))
    qw = N_HEADS_D * Q_SLAB
    return pl.pallas_call(
        _mla_prep_kernel,
        grid=(m // tm,),
        in_specs=[col(C_QL, Q_LORA), col(C_KVL, KV_LORA), col(C_KR, LANE),
                  row(Q_SLAB), row(Q_SLAB), row(LANE), row(LANE),
                  _lw((1, Q_LORA), l), _lw((1, KV_LORA), l), _lw((Q_LORA, qw), l), _lw((Q_LORA, qw), l)],
        out_specs=[row(qw), row(KV_LORA), row(ROPE_DIM), row(LANE)],
        out_shape=[jax.ShapeDtypeStruct((m, qw), BF16), jax.ShapeDtypeStruct((m, KV_LORA), F32),
                   jax.ShapeDtypeStruct((m, ROPE_DIM), F32), jax.ShapeDtypeStruct((m, LANE), F32)],
        compiler_params=_params("parallel"),
        name="mla_prep",
    )(p2d, p2d, p2d, tabs["cos_q"], tabs["sin_q"], tabs["cos_k"], tabs["sin_k"],
      w["mla_q_norm_w"], w["mla_kv_norm_w"], w["wq"], w["wq_sw"])


def _kv_expand_kernel(ckv_ref, kpe_ref, wk_ref, wv_ref, kt_ref, v_ref):
    c = ckv_ref[...].astype(BF16)
    kn = _bdot(c, wk_ref[...])
    v_ref[...] = _bdot(c, wv_ref[...]).astype(v_ref.dtype)
    kr_t = kpe_ref[...].T.astype(kt_ref.dtype)
    for h in range(N_HEADS_D):
        kt_ref[h * Q_SLAB:h * Q_SLAB + NOPE_DIM, :] = kn[:, h * NOPE_DIM:(h + 1) * NOPE_DIM].T.astype(kt_ref.dtype)
        kt_ref[h * Q_SLAB + NOPE_DIM:(h + 1) * Q_SLAB, :] = kr_t


def _kv_expand(ckv, kpe128, w, l):
    b, lk, _ = ckv.shape
    tr = 512 if lk % 512 == 0 else lk
    row = lambda wd: pl.BlockSpec((None, tr, wd), lambda bi, i: (bi, i, 0))
    kw, vw = N_HEADS_D * Q_SLAB, N_HEADS_D * V_DIM
    return pl.pallas_call(
        _kv_expand_kernel,
        grid=(b, lk // tr),
        in_specs=[row(KV_LORA), row(LANE), _lw((KV_LORA, N_HEADS_D * NOPE_DIM), l), _lw((KV_LORA, vw), l)],
        out_specs=[pl.BlockSpec((None, kw, tr), lambda bi, i: (bi, 0, i)), row(vw)],
        out_shape=[jax.ShapeDtypeStruct((b, kw, lk), BF16), jax.ShapeDtypeStruct((b, lk, vw), BF16)],
        compiler_params=_params("parallel", "parallel"),
        name="kv_expand",
    )(ckv, kpe128, w["wk"], w["wv"])


def _attn_kernel(qi_ref, kj_ref, fin_ref, q_ref, kt_ref, v_ref, gd_ref, o_ref, m_s, l_s, acc_s, s_a, s_b, p_a, p_b,
                 *, tq, tk, strip, q_off, lk):
    step = pl.program_id(1)
    j = kj_ref[step]
    q_lo = q_off + qi_ref[step] * tq
    k_lo = j * tk
    all_visible = jnp.minimum((q_lo // CHUNK + 1) * CHUNK, lk)
    s_bufs, p_bufs = (s_a, s_b), (p_a, p_b)

    @pl.when(j == 0)
    def _():
        m_s[...] = jnp.full(m_s.shape, NEG_BIG, F32)
        l_s[...] = jnp.zeros(l_s.shape, F32)
        acc_s[...] = jnp.zeros(acc_s.shape, F32)

    def block(masked):
        if masked:
            qpos = q_lo + lax.broadcasted_iota(jnp.int32, (tq, tk), 0)
            kpos = k_lo + lax.broadcasted_iota(jnp.int32, (tq, tk), 1)
            visible = kpos < jnp.minimum((qpos // CHUNK + 1) * CHUNK, lk)
        def scores(h):
            q = q_ref[:, h * Q_SLAB:(h + 1) * Q_SLAB]
            s = _bdot(q, kt_ref[h * Q_SLAB:(h + 1) * Q_SLAB, :])
            if masked:
                s = jnp.where(visible, s, NEG_BIG)
            s_bufs[h % 2][...] = s
            m_prev = m_s[h]
            m_next = jnp.maximum(m_prev, jnp.max(s, axis=1, keepdims=True))
            m_s[h] = m_next
            return m_next, jnp.exp2(m_prev - m_next)

        def finish(h, m_next, alpha):
            s_buf, p_buf = s_bufs[h % 2], p_bufs[h % 2]
            for r in range(tq // strip):
                rows = slice(r * strip, (r + 1) * strip)
                pr = jnp.exp2(s_buf[rows, :] - m_next[rows, 0:1])
                part = pr[:, 0:LANE]
                for c in range(1, tk // LANE):
                    part = part + pr[:, c * LANE:(c + 1) * LANE]
                l_s[h, rows, :] = alpha[rows] * l_s[h, rows, :] + part
                p_buf[rows, :] = pr.astype(BF16)
            acc_s[h] = acc_s[h] * alpha + _bdot(p_buf[...], v_ref[:, h * V_DIM:(h + 1) * V_DIM])

        stats = scores(0)
        for h in range(N_HEADS_D):
            nxt = scores(h + 1) if h + 1 < N_HEADS_D else None
            finish(h, *stats)
            stats = nxt

    @pl.when(k_lo + tk <= all_visible)
    def _():
        block(False)

    @pl.when(k_lo + tk > all_visible)
    def _():
        block(True)

    @pl.when(fin_ref[step] == 1)
    def _():
        gate = _silu(gd_ref[...])
        for h in range(N_HEADS_D):
            sl = slice(h * V_DIM, (h + 1) * V_DIM)
            denom = jnp.sum(l_s[h], axis=1, keepdims=True)
            o_ref[:, sl] = (acc_s[h] / denom * gate[:, sl]).astype(o_ref.dtype)


def _attention(q, kt, v, p, q_off, lk):
    b, lq, _ = q.shape
    lkp = kt.shape[2]
    tq = _pick(lq, 512)
    tk = 1024 if lkp % 1024 == 0 else lkp
    strip = 16 if tq % 16 == 0 else SUBLANE
    nq = lq // tq
    qi, kj, fin = [], [], []
    for i in range(nq):
        last_bound = min(((q_off + (i + 1) * tq - 1) // CHUNK + 1) * CHUNK, lk)
        last_j = (last_bound - 1) // tk
        for j in range(last_j + 1):
            qi.append(i)
            kj.append(j)
            fin.append(int(j == last_j))
    to_arr = lambda a: jnp.asarray(np.asarray(a, np.int32))
    qw, vw = N_HEADS_D * Q_SLAB, N_HEADS_D * V_DIM
    grid_spec = pltpu.PrefetchScalarGridSpec(
        num_scalar_prefetch=3,
        grid=(b, len(qi)),
        in_specs=[pl.BlockSpec((None, tq, qw), lambda bi, s, qi_r, kj_r, fin_r: (bi, qi_r[s], 0)),
                  pl.BlockSpec((None, qw, tk), lambda bi, s, qi_r, kj_r, fin_r: (bi, 0, kj_r[s])),
                  pl.BlockSpec((None, tk, vw), lambda bi, s, qi_r, kj_r, fin_r: (bi, kj_r[s], 0)),
                  pl.BlockSpec((None, tq, vw), lambda bi, s, qi_r, kj_r, fin_r: (bi, qi_r[s], C_GD // vw))],
        out_specs=pl.BlockSpec((None, tq, vw), lambda bi, s, qi_r, kj_r, fin_r: (bi, qi_r[s], 0)),
        scratch_shapes=[pltpu.VMEM((N_HEADS_D, tq, LANE), F32), pltpu.VMEM((N_HEADS_D, tq, LANE), F32),
                        pltpu.VMEM((N_HEADS_D, tq, V_DIM), F32), pltpu.VMEM((tq, tk), F32), pltpu.VMEM((tq, tk), F32),
                        pltpu.VMEM((tq, tk), BF16), pltpu.VMEM((tq, tk), BF16)],
    )
    return pl.pallas_call(
        functools.partial(_attn_kernel, tq=tq, tk=tk, strip=strip, q_off=q_off, lk=lk),
        grid_spec=grid_spec,
        out_shape=jax.ShapeDtypeStruct((b, lq, vw), BF16),
        compiler_params=_params("parallel", "arbitrary"),
        name="mla_attention",
    )(to_arr(qi), to_arr(kj), to_arr(fin), q, kt, v, p)


def _prep_weights(pre_norm_w, w_in, conv_a_w, conv_a_b, lru_w_r, lru_b_r, lru_w_i, lru_b_i, lru_lambda, conv_b_w,
                  conv_c_w, conv_c_b, ssd_dt_bias, ssd_a_log, ssd_d, ssd_norm_w, mla_q_norm_w, mla_w_q_up,
                  mla_kv_norm_w, mla_w_kv_up, w_branch_out, w_out, post_norm_w):
    d = w_in.shape[0]
    w_all = _pack_w_in(w_in)

    wq = mla_w_q_up.reshape(d, Q_LORA, N_HEADS_D, NOPE_DIM + ROPE_DIM)
    nope, x1, x2 = wq[..., :NOPE_DIM], wq[..., NOPE_DIM:NOPE_DIM + HALF], wq[..., NOPE_DIM + HALF:]
    zq = lambda n: jnp.zeros((d, Q_LORA, N_HEADS_D, n), wq.dtype)
    wq_main = jnp.concatenate([nope, x1, x2, zq(Q_SLAB - NOPE_DIM - ROPE_DIM)], -1).reshape(d, Q_LORA, -1).astype(BF16)
    wq_sw = jnp.concatenate([zq(NOPE_DIM), -x2, x1, zq(Q_SLAB - NOPE_DIM - ROPE_DIM)], -1).reshape(d, Q_LORA, -1).astype(BF16)
    wkv = mla_w_kv_up.reshape(d, KV_LORA, N_HEADS_D, NOPE_DIM + V_DIM)
    wk = wkv[..., :NOPE_DIM].reshape(d, KV_LORA, -1).astype(BF16)
    wv = wkv[..., NOPE_DIM:].reshape(d, KV_LORA, -1).astype(BF16)

    pad_heads = lambda a: jnp.pad(a, ((0, 0), (0, LANE - N_HEADS_C)))[:, None, :]
    expand = (jnp.arange(LANE)[:, None] == (jnp.arange(BRANCH_W) // HEAD_DIM_C)[None, :]).astype(BF16)
    vec = lambda a: a[:, None, :]
    return dict(
        pre_norm_w=vec(pre_norm_w), post_norm_w=vec(post_norm_w), w_all=w_all,
        w_branch=w_branch_out.astype(BF16), w_out=w_out.astype(BF16),
        conv_a_w=conv_a_w, conv_a_b=vec(conv_a_b), lru_w_r=lru_w_r.astype(BF16), lru_b_r=vec(lru_b_r),
        lru_w_i=lru_w_i.astype(BF16), lru_b_i=vec(lru_b_i), lru_lambda=vec(lru_lambda), conv_b_w=conv_b_w,
        conv_cx_w=conv_c_w[..., :BRANCH_W], conv_cx_b=vec(conv_c_b[..., :BRANCH_W]),
        conv_cb_w=conv_c_w[..., BRANCH_W:], conv_cb_b=vec(conv_c_b[..., BRANCH_W:]),
        dt_bias=pad_heads(ssd_dt_bias), a_log=pad_heads(ssd_a_log), d_exp=vec(jnp.repeat(ssd_d, HEAD_DIM_C, axis=-1)),
        ssd_norm_w=vec(ssd_norm_w), expand=expand,
        mla_q_norm_w=vec(mla_q_norm_w), mla_kv_norm_w=vec(mla_kv_norm_w), wq=wq_main, wq_sw=wq_sw, wk=wk, wv=wv,
    )


def _rope_tables(pos, reps):
    inv = ROPE_THETA ** (-jnp.arange(HALF, dtype=F32) / HALF)
    ang = pos.astype(F32)[:, None] * inv[None, :]
    cos, sin = jnp.cos(ang), jnp.sin(ang)
    n = pos.shape[0]
    z = lambda w: jnp.zeros((n, w), F32)
    tabs = dict(
        cos_q=jnp.concatenate([jnp.ones((n, NOPE_DIM), F32), cos, cos, z(Q_SLAB - NOPE_DIM - ROPE_DIM)], 1),
        sin_q=jnp.concatenate([z(NOPE_DIM), sin, sin, z(Q_SLAB - NOPE_DIM - ROPE_DIM)], 1),
        cos_k=jnp.concatenate([cos, cos, z(LANE - ROPE_DIM)], 1),
        sin_k=jnp.concatenate([-sin, sin, z(LANE - ROPE_DIM)], 1),
    )
    return {k: jnp.tile(v, (reps, 1)) for k, v in tabs.items()}


def _layer(x, h, b, seq, tabs, q_off, state, past, w, l, last):
    conv_a_prev, h0, conv_b_prev, conv_c_prev, ssd0 = state
    m = b * seq
    p2d = _matmul(h, w["w_all"], l, PACK_W, 1024, F32, "in_proj")
    p = p2d.reshape(b, seq, PACK_W)

    y_a, h_last, conv_a_new = _lru(p, conv_a_prev, h0[:, None, :], w, l)
    y_b, conv_b_new = _sconv(p, conv_b_prev, w, l)
    y_c, conv_cx_new, conv_cb_new, ssd_t = _ssd(p, conv_c_prev[..., :BRANCH_W], conv_c_prev[..., BRANCH_W:],
                                                 jnp.swapaxes(ssd0, -1, -2), w, l)
    q, ckv, kpe, kpe128 = _mla_prep(p2d, tabs, w, l)
    ckv_all, kpe_all, lk = ckv.reshape(b, seq, KV_LORA), kpe128.reshape(b, seq, LANE), seq
    if past is not None:
        past_ckv, past_kpe = past
        lk = past_ckv.shape[1] + seq
        lkp = -(-lk // LANE) * LANE
        tail = lambda wd: jnp.zeros((b, lkp - lk, wd), F32)
        ckv_all = jnp.concatenate([past_ckv, ckv_all, tail(KV_LORA)], axis=1)
        past_kpe128 = jnp.pad(past_kpe, ((0, 0), (0, 0), (0, LANE - ROPE_DIM)))
        kpe_all = jnp.concatenate([past_kpe128, kpe_all, tail(LANE)], axis=1)
    kt, v = _kv_expand(ckv_all, kpe_all, w, l)
    y_d = _attention(q.reshape(b, seq, -1), kt, v, p, q_off, lk)

    merged = _merge(h, [y.reshape(m, BRANCH_W) for y in (y_a, y_b, y_c, y_d)], w["w_all"], w["w_branch"], l)
    out = _matmul(merged, w["w_out"], l, D_MODEL, 1024, F32, "out_proj")
    x_new, h_next = _post(x, out, w["post_norm_w"], w["pre_norm_w"], l, last)
    new_state = (ckv.reshape(b, seq, KV_LORA), kpe.reshape(b, seq, ROPE_DIM), conv_a_new, h_last[:, 0, :], conv_b_new,
                 jnp.concatenate([conv_cx_new, conv_cb_new], axis=-1), jnp.swapaxes(ssd_t, -1, -2))
    return x_new, h_next, new_state


def kernel(x_prompt, x_sample, cache_mla_latent, cache_mla_kpe, state_lru_conv, state_lru_h, state_sconv, state_ssd_conv, state_ssd, pre_norm_w, w_in, conv_a_w, conv_a_b, lru_w_r, lru_b_r, lru_w_i, lru_b_i, lru_lambda, conv_b_w, conv_c_w, conv_c_b, ssd_dt_bias, ssd_a_log, ssd_d, ssd_norm_w, mla_q_norm_w, mla_w_q_up, mla_kv_norm_w, mla_w_kv_up, w_branch_out, w_out, post_norm_w):
    bp, sp, d = x_prompt.shape
    bs, ss, _ = x_sample.shape
    past = cache_mla_latent.shape[2]
    depth = w_in.shape[0]
    w = _prep_weights(pre_norm_w, w_in, conv_a_w, conv_a_b, lru_w_r, lru_b_r, lru_w_i, lru_b_i, lru_lambda, conv_b_w,
                      conv_c_w, conv_c_b, ssd_dt_bias, ssd_a_log, ssd_d, ssd_norm_w, mla_q_norm_w, mla_w_q_up,
                      mla_kv_norm_w, mla_w_kv_up, w_branch_out, w_out, post_norm_w)
    tabs_p = _rope_tables(jnp.arange(sp), bp)
    tabs_s = _rope_tables(past + jnp.arange(ss), bs)
    zeros_p = (jnp.zeros((bp, 3, BRANCH_W), F32), jnp.zeros((bp, BRANCH_W), F32), jnp.zeros((bp, 2, BRANCH_W), F32),
               jnp.zeros((bp, 3, BRANCH_W + BC_W), F32), jnp.zeros((bp, N_HEADS_C, HEAD_DIM_C, D_STATE), F32))

    xp = x_prompt.reshape(bp * sp, d)
    xs = x_sample.reshape(bs * ss, d)
    hp = _rms_cast(xp, w["pre_norm_w"], 0)
    hs = _rms_cast(xs, w["pre_norm_w"], 0)
    p_st = [[] for _ in range(7)]
    s_st = [[] for _ in range(7)]
    for l in range(depth):
        last = l == depth - 1
        xp, hp, new_p = _layer(xp, hp, bp, sp, tabs_p, 0, zeros_p, None, w, l, last)
        st_s = (state_lru_conv[l], state_lru_h[l], state_sconv[l], state_ssd_conv[l], state_ssd[l])
        xs, hs, new_s = _layer(xs, hs, bs, ss, tabs_s, past, st_s, (cache_mla_latent[l], cache_mla_kpe[l]), w, l, last)
        for i in range(7):
            p_st[i].append(new_p[i])
            s_st[i].append(new_s[i])
    p_out = [jnp.stack(v, axis=0) for v in p_st]
    s_out = [jnp.stack(v, axis=0) for v in s_st]
    return (xp.reshape(bp, sp, d), xs.reshape(bs, ss, d), *p_out, *s_out)
```

```python
import functools

import numpy as np
import jax
import jax.numpy as jnp
from jax import lax
from jax.experimental import pallas as pl
from jax.experimental.pallas import tpu as pltpu

F32 = jnp.float32
BF16 = jnp.bfloat16

D_MODEL = 4096
CHUNK = 64
BRANCH_W = 1024
N_BRANCH = 4
EPS = 1e-6
LRU_BLOCKS = 8
LRU_BLOCK = 128
LRU_C = 8.0
N_HEADS_C = 16
HEAD_DIM_C = 64
N_GROUPS_C = 2
D_STATE = 128
N_HEADS_D = 8
NOPE_DIM = 128
ROPE_DIM = 64
V_DIM = 128
Q_LORA = 1024
KV_LORA = 512
ROPE_THETA = 10000.0
ATTN_SCALE = (NOPE_DIM + ROPE_DIM) ** -0.5
HALF = ROPE_DIM // 2
BC_W = 2 * N_GROUPS_C * D_STATE

LANE = 128
SUBLANE = 8
VMEM_LIMIT = 56 * 1024 * 1024

O_BC = 7168
O_DT = 7680
O_ZC = 7696
O_QL = 8720
O_KVL = 9744
O_KPE = 10256
O_GD = 10320
O_MG = 11344
PACK_TILE = 512
PACK_SRC = (tuple(range(0, O_BC, PACK_TILE)) + (O_ZC, O_ZC + PACK_TILE) + (O_QL, O_QL + PACK_TILE)
            + (O_GD, O_GD + PACK_TILE) + (O_BC, O_KVL))
PACK_W = len(PACK_SRC) * PACK_TILE
C_XA, C_GA, C_BB, C_CB, C_XB, C_GB, C_XC, C_ZC, C_QL, C_GD = (i * BRANCH_W for i in range(10))
C_BCC = 10240
C_KVL = 10752
NARROW_SRC = (O_KPE, O_DT)
C_KR = 0
C_DT = LANE
Q_SLAB = 256

NT_DIMS = (((1,), (1,)), ((), ()))
HI = lax.Precision.HIGHEST
EXP2_SCALE = ATTN_SCALE * 1.4426950408889634
NEG_BIG = -1e30


def _pick(n, pref):
    if n <= pref:
        return n
    t = pref
    while n % t:
        t -= SUBLANE
    return t


def _params(*sem):
    return pltpu.CompilerParams(dimension_semantics=sem, vmem_limit_bytes=VMEM_LIMIT)


def _lw(shape, l):
    idx = (l,) + (0,) * len(shape)
    return pl.BlockSpec((None,) + tuple(shape), lambda *_: idx)


def _sigmoid(x):
    return 1.0 / (1.0 + jnp.exp(-x))


def _silu(x):
    return x * _sigmoid(x)


def _softplus(x):
    return jnp.maximum(x, 0.0) + jnp.log1p(jnp.exp(-jnp.abs(x)))


def _neg_expm1(y):
    e = jnp.exp(y)
    near = jnp.where(e == 1.0, -y, (1.0 - e) * y / jnp.log(e))
    return jnp.where(y > -0.5, near, 1.0 - e)


def _rms(x, w):
    return x * lax.rsqrt(jnp.mean(x * x, axis=-1, keepdims=True) + EPS) * w


def _bdot(a, b):
    return jnp.dot(a, b, preferred_element_type=F32)


def _sum3(dot_with, x):
    x1 = x.astype(BF16)
    r1 = x - x1.astype(F32)
    x2 = r1.astype(BF16)
    x3 = (r1 - x2.astype(F32)).astype(BF16)
    return dot_with(x1) + dot_with(x2) + dot_with(x3)


def _pack_kernel(off_ref, a_ref, o_ref):
    del off_ref
    o_ref[...] = a_ref[0].T.astype(o_ref.dtype)


def _pack_tiles(wt, src, tile, name):
    d, _, k = wt.shape
    grid_spec = pltpu.PrefetchScalarGridSpec(
        num_scalar_prefetch=1,
        grid=(d, len(src)),
        in_specs=[pl.BlockSpec((pl.Element(1), pl.Element(tile), pl.Element(k)), lambda l, j, off: (l, pl.multiple_of(off[j], 16), 0))],
        out_specs=pl.BlockSpec((None, k, tile), lambda l, j, off: (l, 0, j)),
    )
    return pl.pallas_call(
        _pack_kernel, grid_spec=grid_spec, out_shape=jax.ShapeDtypeStruct((d, k, len(src) * tile), BF16),
        compiler_params=_params("parallel", "parallel"), name=name,
    )(jnp.asarray(np.asarray(src, np.int32)), wt)


def _pack_w_in(w_in):
    wt = jnp.swapaxes(w_in, 1, 2)
    wide = _pack_tiles(wt, PACK_SRC + tuple(range(O_MG, w_in.shape[2], PACK_TILE)), PACK_TILE, "pack_w_in")
    return wide, _pack_tiles(wt, NARROW_SRC, LANE, "pack_w_in_narrow")


def _rms_cast_kernel(x_ref, w_ref, o_ref):
    o_ref[...] = _rms(x_ref[...], w_ref[...]).astype(o_ref.dtype)


def _rms_cast(x, w, l):
    m, d = x.shape
    tm = _pick(m, 256)
    return pl.pallas_call(
        _rms_cast_kernel,
        grid=(m // tm,),
        in_specs=[pl.BlockSpec((tm, d), lambda i: (i, 0)), _lw((1, d), l)],
        out_specs=pl.BlockSpec((tm, d), lambda i: (i, 0)),
        out_shape=jax.ShapeDtypeStruct((m, d), BF16),
        compiler_params=_params("parallel"),
        name="rms_cast",
    )(x, w)


def _mm_kernel(a_ref, w_ref, o_ref):
    o_ref[...] = _bdot(a_ref[...], w_ref[...]).astype(o_ref.dtype)


def _matmul(a, w, l, ncols, tn, out_dtype, name):
    m, k = a.shape
    tm = _pick(m, 1024)
    return pl.pallas_call(
        _mm_kernel,
        grid=(m // tm, ncols // tn),
        in_specs=[pl.BlockSpec((tm, k), lambda i, j: (i, 0)), pl.BlockSpec((None, k, tn), lambda i, j: (l, 0, j))],
        out_specs=pl.BlockSpec((tm, tn), lambda i, j: (i, j)),
        out_shape=jax.ShapeDtypeStruct((m, ncols), out_dtype),
        compiler_params=_params("parallel", "arbitrary"),
        name=name,
    )(a, w)


def _merge_kernel(h_ref, ya_ref, yb_ref, yc_ref, yd_ref, g0, g1, g2, g3, b0, b1, b2, b3, o_ref):
    h = h_ref[...]
    acc = None
    for y_ref, g_ref, b_ref in ((ya_ref, g0, b0), (yb_ref, g1, b1), (yc_ref, g2, b2), (yd_ref, g3, b3)):
        term = _sigmoid(_bdot(h, g_ref[...])) * _bdot(y_ref[...], b_ref[...])
        acc = term if acc is None else acc + term
    o_ref[...] = acc.astype(o_ref.dtype)


def _merge(h, ys, w_all, wb, l):
    m = h.shape[0]
    tm = _pick(m, 1024)
    tn = 256
    once = pl.Buffered(1)
    y_spec = pl.BlockSpec((tm, BRANCH_W), lambda i, j: (i, 0), pipeline_mode=once)
    gate0 = PACK_W // tn
    g_specs = [pl.BlockSpec((None, D_MODEL, tn), functools.partial(lambda i, j, n: (l, 0, gate0 + n * (D_MODEL // tn) + j), n=n))
               for n in range(N_BRANCH)]
    b_specs = [pl.BlockSpec((None, None, BRANCH_W, tn), functools.partial(lambda i, j, n: (l, n, 0, j), n=n)) for n in range(N_BRANCH)]
    return pl.pallas_call(
        _merge_kernel,
        grid=(m // tm, D_MODEL // tn),
        in_specs=[pl.BlockSpec((tm, D_MODEL), lambda i, j: (i, 0), pipeline_mode=once)] + [y_spec] * 4 + g_specs + b_specs,
        out_specs=pl.BlockSpec((tm, tn), lambda i, j: (i, j)),
        out_shape=jax.ShapeDtypeStruct((m, D_MODEL), BF16),
        compiler_params=_params("parallel", "arbitrary"),
        name="gate_merge",
    )(h, *ys, w_all, w_all, w_all, w_all, wb, wb, wb, wb)


def _post_kernel(x_ref, o_ref, pw_ref, nw_ref, xn_ref, hn_ref):
    xn = x_ref[...] + _rms(o_ref[...], pw_ref[...])
    xn_ref[...] = xn
    hn_ref[...] = _rms(xn, nw_ref[...]).astype(hn_ref.dtype)


def _post_last_kernel(x_ref, o_ref, pw_ref, xn_ref):
    xn_ref[...] = x_ref[...] + _rms(o_ref[...], pw_ref[...])


def _post(x, out, post_w, pre_w, l, last):
    m, d = x.shape
    tm = _pick(m, 256)
    row = pl.BlockSpec((tm, d), lambda i: (i, 0))
    if last:
        return pl.pallas_call(
            _post_last_kernel, grid=(m // tm,), in_specs=[row, row, _lw((1, d), l)], out_specs=row,
            out_shape=jax.ShapeDtypeStruct((m, d), F32), compiler_params=_params("parallel"), name="post_last",
        )(x, out, post_w), None
    return pl.pallas_call(
        _post_kernel, grid=(m // tm,), in_specs=[row, row, _lw((1, d), l), _lw((1, d), l + 1)], out_specs=[row, row],
        out_shape=[jax.ShapeDtypeStruct((m, d), F32), jax.ShapeDtypeStruct((m, d), BF16)],
        compiler_params=_params("parallel"), name="post",
    )(x, out, post_w, pre_w)


def _conv_tile(ext_s, x, w_ref, width, tl):
    ext_s[8:8 + tl, :] = x
    k0 = 9 - width
    y = ext_s[k0:k0 + tl, :] * w_ref[0:1, :]
    for j in range(1, width - 1):
        y = y + ext_s[k0 + j:k0 + j + tl, :] * w_ref[j:j + 1, :]
    return y + x * w_ref[width - 1:width, :]


def _conv_carry(ext_s, width, tl):
    tail = ext_s[tl + 9 - width:tl + 8, :]
    ext_s[9 - width:8, :] = tail
    return tail


def _lru_kernel(xa_ref, ga_ref, cprev_ref, h0_ref, cw_ref, cb_ref, wr_ref, br_ref, wi_ref, bi_ref, lam_ref,
                y_ref, hlast_ref, cnew_ref, ext_s, a_s, u_s, h_s, *, tl, nl):
    l = pl.program_id(1)

    @pl.when(l == 0)
    def _():
        ext_s[5:8, :] = cprev_ref[...]
        h_s[...] = h0_ref[...]

    xc = _conv_tile(ext_s, xa_ref[...], cw_ref, 4, tl) + cb_ref[...]
    tail = _conv_carry(ext_s, 4, tl)

    xcb = xc.astype(BF16)
    rs, gs = [], []
    for n in range(LRU_BLOCKS):
        blk = xcb[:, n * LRU_BLOCK:(n + 1) * LRU_BLOCK]
        rs.append(_bdot(blk, wr_ref[n]))
        gs.append(_bdot(blk, wi_ref[n]))
    r = _sigmoid(jnp.concatenate(rs, axis=1) + br_ref[...])
    gate_i = _sigmoid(jnp.concatenate(gs, axis=1) + bi_ref[...])
    log_a = (-LRU_C * r) * _softplus(-lam_ref[...])
    a_s[...] = jnp.exp(log_a)
    u_s[...] = jnp.sqrt(_neg_expm1(2.0 * log_a)) * (gate_i * xc)

    def step(t, h):
        h = a_s[pl.ds(t, 1), :] * h + u_s[pl.ds(t, 1), :]
        u_s[pl.ds(t, 1), :] = h
        return h

    h = lax.fori_loop(0, tl, step, h_s[...], unroll=8)
    h_s[...] = h
    y_ref[...] = (u_s[...] * _silu(ga_ref[...])).astype(y_ref.dtype)

    @pl.when(l == nl - 1)
    def _():
        hlast_ref[...] = h
        cnew_ref[...] = tail


def _lru(p, cprev, h0, w, l):
    b, seq, _ = p.shape
    tl = _pick(seq, 256)
    nl = seq // tl
    col = lambda c: pl.BlockSpec((None, tl, BRANCH_W), lambda i, t: (i, t, c // BRANCH_W))
    st3 = pl.BlockSpec((None, 3, BRANCH_W), lambda i, t: (i, 0, 0))
    st1 = pl.BlockSpec((None, 1, BRANCH_W), lambda i, t: (i, 0, 0))
    vec = _lw((1, BRANCH_W), l)
    blk = _lw((LRU_BLOCKS, LRU_BLOCK, LRU_BLOCK), l)
    return pl.pallas_call(
        functools.partial(_lru_kernel, tl=tl, nl=nl),
        grid=(b, nl),
        in_specs=[col(C_XA), col(C_GA), st3, st1, _lw((4, BRANCH_W), l), vec, blk, vec, blk, vec, vec],
        out_specs=[pl.BlockSpec((None, tl, BRANCH_W), lambda i, t: (i, t, 0)), st1, st3],
        out_shape=[jax.ShapeDtypeStruct((b, seq, BRANCH_W), BF16), jax.ShapeDtypeStruct((b, 1, BRANCH_W), F32),
                   jax.ShapeDtypeStruct((b, 3, BRANCH_W), F32)],
        scratch_shapes=[pltpu.VMEM((tl + 8, BRANCH_W), F32), pltpu.VMEM((tl, BRANCH_W), F32),
                        pltpu.VMEM((tl, BRANCH_W), F32), pltpu.VMEM((1, BRANCH_W), F32)],
        compiler_params=_params("parallel", "arbitrary"),
        name="rglru",
    )(p, p, cprev, h0, w["conv_a_w"], w["conv_a_b"], w["lru_w_r"], w["lru_b_r"], w["lru_w_i"], w["lru_b_i"], w["lru_lambda"])


def _sconv_kernel(bb_ref, cb_ref, xb_ref, gb_ref, cprev_ref, cw_ref, y_ref, cnew_ref, ext_s, *, tl, nl):
    l = pl.program_id(1)

    @pl.when(l == 0)
    def _():
        ext_s[6:8, :] = cprev_ref[...]

    v = _conv_tile(ext_s, cb_ref[...] * xb_ref[...], cw_ref, 3, tl)
    tail = _conv_carry(ext_s, 3, tl)
    y_ref[...] = (bb_ref[...] * v * _silu(gb_ref[...])).astype(y_ref.dtype)

    @pl.when(l == nl - 1)
    def _():
        cnew_ref[...] = tail


def _sconv(p, cprev, w, l):
    b, seq, _ = p.shape
    tl = _pick(seq, 256)
    nl = seq // tl
    col = lambda c: pl.BlockSpec((None, tl, BRANCH_W), lambda i, t: (i, t, c // BRANCH_W))
    st2 = pl.BlockSpec((None, 2, BRANCH_W), lambda i, t: (i, 0, 0))
    return pl.pallas_call(
        functools.partial(_sconv_kernel, tl=tl, nl=nl),
        grid=(b, nl),
        in_specs=[col(C_BB), col(C_CB), col(C_XB), col(C_GB), st2, _lw((3, BRANCH_W), l)],
        out_specs=[pl.BlockSpec((None, tl, BRANCH_W), lambda i, t: (i, t, 0)), st2],
        out_shape=[jax.ShapeDtypeStruct((b, seq, BRANCH_W), BF16), jax.ShapeDtypeStruct((b, 2, BRANCH_W), F32)],
        scratch_shapes=[pltpu.VMEM((tl + 8, BRANCH_W), F32)],
        compiler_params=_params("parallel", "arbitrary"),
        name="sconv",
    )(p, p, p, p, cprev, w["conv_b_w"])


def _ssd_kernel(xc_ref, bcc_ref, dt_ref, zc_ref, cpx_ref, cpb_ref, st0_ref, cwx_ref, cbx_ref, cwb_ref, cbb_ref,
                dtb_ref, alog_ref, dexp_ref, nw_ref, e_ref,
                y_ref, cnx_ref, cnb_ref, st_ref, extx_s, extb_s, st_s, *, q, nc, nl):
    t = q * nc
    l = pl.program_id(1)

    @pl.when(l == 0)
    def _():
        extx_s[5:8, :] = cpx_ref[...]
        extb_s[5:8, :] = cpb_ref[...]
        st_s[...] = st0_ref[...]

    xs_all = _silu(_conv_tile(extx_s, xc_ref[...], cwx_ref, 4, t) + cbx_ref[...])
    tailx = _conv_carry(extx_s, 4, t)
    bcs_all = _silu(_conv_tile(extb_s, bcc_ref[...], cwb_ref, 4, t) + cbb_ref[...])
    tailb = _conv_carry(extb_s, 4, t)

    dt_all = _softplus(dt_ref[...] + dtb_ref[...])
    da_all = dt_all * (-jnp.exp(alog_ref[...]))
    gate_all = _silu(zc_ref[...])
    rows = lax.broadcasted_iota(jnp.int32, (q, q), 0)
    cols = lax.broadcasted_iota(jnp.int32, (q, q), 1)
    tri = rows >= cols
    eye = (lax.broadcasted_iota(jnp.int32, (LANE, LANE), 0) == lax.broadcasted_iota(jnp.int32, (LANE, LANE), 1))
    eye_b = eye.astype(BF16)
    expand = e_ref[...]

    def chunk(r0):
        xs, bcs = xs_all[r0:r0 + q, :], bcs_all[r0:r0 + q, :]
        dt, d_a = dt_all[r0:r0 + q, :], da_all[r0:r0 + q, :]
        acum = _sum3(lambda part: _bdot(tri.astype(BF16), part), d_a)
        acum_t = _sum3(lambda part: lax.dot_general(eye_b, part, NT_DIMS, preferred_element_type=F32), acum)
        dt_x = _sum3(lambda part: _bdot(part, expand), dt)
        acum_x = _sum3(lambda part: _bdot(part, expand), acum)
        end_x = acum_x[q - 1:q, :]
        xdt = xs * dt_x
        xw = (xdt * jnp.exp(end_x - acum_x)).astype(BF16)
        xdt_b = xdt.astype(BF16)
        eacum_x = jnp.exp(acum_x)
        cdecay_x = jnp.exp(end_x)

        y_parts = []
        for g in range(N_GROUPS_C):
            bg = bcs[:, g * D_STATE:(g + 1) * D_STATE].astype(BF16)
            cg = bcs[:, (N_GROUPS_C + g) * D_STATE:(N_GROUPS_C + g + 1) * D_STATE].astype(BF16)
            cb = lax.dot_general(cg, bg, NT_DIMS, preferred_element_type=F32)
            bg_t = lax.dot_general(eye_b, bg, NT_DIMS, preferred_element_type=F32).astype(BF16)
            for e in range(N_HEADS_C // N_GROUPS_C):
                h = g * (N_HEADS_C // N_GROUPS_C) + e
                lo = h * HEAD_DIM_C
                seg = acum_x[:, lo:lo + q] - acum_t[h:h + 1, :]
                decay = jnp.exp(jnp.where(tri, seg, -jnp.inf))
                y_diag = _bdot((cb * decay).astype(BF16), xdt_b[:, lo:lo + HEAD_DIM_C])
                prev = st_s[h]
                y_off = _bdot(cg, prev.astype(BF16)) * eacum_x[:, lo:lo + HEAD_DIM_C]
                st_s[h] = prev * cdecay_x[:, lo:lo + HEAD_DIM_C] + _bdot(bg_t, xw[:, lo:lo + HEAD_DIM_C])
                y_parts.append(y_diag + y_off)
        y = jnp.concatenate(y_parts, axis=1) + dexp_ref[...] * xs

        gated = y * gate_all[r0:r0 + q, :]
        half = BRANCH_W // N_GROUPS_C
        normed = jnp.concatenate(
            [gated[:, i * half:(i + 1) * half]
             * lax.rsqrt(jnp.mean(gated[:, i * half:(i + 1) * half] ** 2, axis=-1, keepdims=True) + EPS)
             for i in range(N_GROUPS_C)], axis=1)
        y_ref[r0:r0 + q, :] = (normed * nw_ref[...]).astype(y_ref.dtype)

    for ci in range(nc):
        chunk(ci * q)

    @pl.when(l == nl - 1)
    def _():
        cnx_ref[...] = tailx
        cnb_ref[...] = tailb
        st_ref[...] = st_s[...]


def _ssd(p, pn, cprev_x, cprev_b, st0, w, l):
    b, seq, _ = p.shape
    q = CHUNK if seq % CHUNK == 0 else seq
    nc = 4 if (seq // q) % 4 == 0 else 1
    tt = q * nc
    nl = seq // tt
    st_shape = (N_HEADS_C, D_STATE, HEAD_DIM_C)
    colx = lambda c, wd: pl.BlockSpec((None, tt, wd), lambda i, t: (i, t, c // wd))
    st3 = lambda wd: pl.BlockSpec((None, 3, wd), lambda i, t: (i, 0, 0))
    st_spec = pl.BlockSpec((None,) + st_shape, lambda i, t: (i, 0, 0, 0))
    return pl.pallas_call(
        functools.partial(_ssd_kernel, q=q, nc=nc, nl=nl),
        grid=(b, nl),
        in_specs=[colx(C_XC, BRANCH_W), colx(C_BCC, BC_W), colx(C_DT, LANE), colx(C_ZC, BRANCH_W),
                  st3(BRANCH_W), st3(BC_W), st_spec,
                  _lw((4, BRANCH_W), l), _lw((1, BRANCH_W), l), _lw((4, BC_W), l), _lw((1, BC_W), l),
                  _lw((1, LANE), l), _lw((1, LANE), l), _lw((1, BRANCH_W), l), _lw((1, BRANCH_W), l),
                  pl.BlockSpec((LANE, BRANCH_W), lambda i, t: (0, 0))],
        out_specs=[pl.BlockSpec((None, tt, BRANCH_W), lambda i, t: (i, t, 0)), st3(BRANCH_W), st3(BC_W), st_spec],
        out_shape=[jax.ShapeDtypeStruct((b, seq, BRANCH_W), BF16), jax.ShapeDtypeStruct((b, 3, BRANCH_W), F32),
                   jax.ShapeDtypeStruct((b, 3, BC_W), F32), jax.ShapeDtypeStruct((b,) + st_shape, F32)],
        scratch_shapes=[pltpu.VMEM((tt + 8, BRANCH_W), F32), pltpu.VMEM((tt + 8, BC_W), F32), pltpu.VMEM(st_shape, F32)],
        compiler_params=_params("parallel", "arbitrary"),
        name="ssd",
    )(p, p, pn, p, cprev_x, cprev_b, st0, w["conv_cx_w"], w["conv_cx_b"], w["conv_cb_w"], w["conv_cb_b"],
      w["dt_bias"], w["a_log"], w["d_exp"], w["ssd_norm_w"], w["expand"])


def _mla_prep_kernel(ql_ref, kvl_ref, kraw_ref, ck_ref, sk_ref, qnw_ref, kvnw_ref, wq_ref,
                     q_ref, ckv_ref, kpe_ref, kpe128_ref):
    cos_k, sin_k = ck_ref[...], sk_ref[...]
    lane = lax.broadcasted_iota(jnp.int32, cos_k.shape, 1)

    def rope(x):
        swapped = jnp.where(lane < HALF, pltpu.roll(x, LANE - HALF, axis=1), pltpu.roll(x, HALF, axis=1))
        return x * cos_k + swapped * sin_k

    qn = _rms(ql_ref[...], qnw_ref[...]).astype(BF16)
    q = _bdot(qn, wq_ref[...])
    for h in range(N_HEADS_D):
        lo = h * Q_SLAB
        q_ref[:, lo:lo + NOPE_DIM] = (q[:, lo:lo + NOPE_DIM] * EXP2_SCALE).astype(q_ref.dtype)
        q_ref[:, lo + NOPE_DIM:lo + Q_SLAB] = (rope(q[:, lo + NOPE_DIM:lo + Q_SLAB]) * EXP2_SCALE).astype(q_ref.dtype)
    ckv_ref[...] = _rms(kvl_ref[...], kvnw_ref[...])
    kr = rope(kraw_ref[...])
    kpe128_ref[...] = kr
    kpe_ref[...] = kr[:, :ROPE_DIM]


def _mla_prep(p2d, pn2d, tabs, w, l):
    m = p2d.shape[0]
    tm = _pick(m, 256)
    col = lambda c, wd: pl.BlockSpec((tm, wd), lambda i: (i, c // wd))
    row = lambda wd: pl.BlockSpec((tm, wd), lambda i: (i, 0))
    qw = N_HEADS_D * Q_SLAB
    return pl.pallas_call(
        _mla_prep_kernel,
        grid=(m // tm,),
        in_specs=[col(C_QL, Q_LORA), col(C_KVL, KV_LORA), col(C_KR, LANE), row(LANE), row(LANE),
                  _lw((1, Q_LORA), l), _lw((1, KV_LORA), l), _lw((Q_LORA, qw), l)],
        out_specs=[row(qw), row(KV_LORA), row(ROPE_DIM), row(LANE)],
        out_shape=[jax.ShapeDtypeStruct((m, qw), BF16), jax.ShapeDtypeStruct((m, KV_LORA), F32),
                   jax.ShapeDtypeStruct((m, ROPE_DIM), F32), jax.ShapeDtypeStruct((m, LANE), F32)],
        compiler_params=_params("parallel"),
        name="mla_prep",
    )(p2d, p2d, pn2d, tabs["cos_k"], tabs["sin_k"], w["mla_q_norm_w"], w["mla_kv_norm_w"], w["wq"])


def _kv_expand_kernel(ckv_ref, kpe_ref, wk_ref, wv_ref, kt_ref, v_ref):
    c = ckv_ref[...].astype(BF16)
    kn = _bdot(c, wk_ref[...])
    v_ref[...] = _bdot(c, wv_ref[...]).astype(v_ref.dtype)
    kr_t = kpe_ref[...].T.astype(kt_ref.dtype)
    for h in range(N_HEADS_D):
        kt_ref[h * Q_SLAB:h * Q_SLAB + NOPE_DIM, :] = kn[:, h * NOPE_DIM:(h + 1) * NOPE_DIM].T.astype(kt_ref.dtype)
        kt_ref[h * Q_SLAB + NOPE_DIM:(h + 1) * Q_SLAB, :] = kr_t


def _kv_expand(ckv, kpe128, w, l):
    b, lk, _ = ckv.shape
    tr = 512 if lk % 512 == 0 else lk
    row = lambda wd: pl.BlockSpec((None, tr, wd), lambda bi, i: (bi, i, 0))
    kw, vw = N_HEADS_D * Q_SLAB, N_HEADS_D * V_DIM
    return pl.pallas_call(
        _kv_expand_kernel,
        grid=(b, lk // tr),
        in_specs=[row(KV_LORA), row(LANE), _lw((KV_LORA, N_HEADS_D * NOPE_DIM), l), _lw((KV_LORA, vw), l)],
        out_specs=[pl.BlockSpec((None, kw, tr), lambda bi, i: (bi, 0, i)), row(vw)],
        out_shape=[jax.ShapeDtypeStruct((b, kw, lk), BF16), jax.ShapeDtypeStruct((b, lk, vw), BF16)],
        compiler_params=_params("parallel", "parallel"),
        name="kv_expand",
    )(ckv, kpe128, w["wk"], w["wv"])


def _attn_kernel(qi_ref, kj_ref, fin_ref, q_ref, kt_ref, v_ref, gd_ref, o_ref, m_s, l_s, acc_s, s_a, s_b, p_a, p_b,
                 *, tq, tk, strip, q_off, lk):
    step = pl.program_id(1)
    j = kj_ref[step]
    q_lo = q_off + qi_ref[step] * tq
    k_lo = j * tk
    all_visible = jnp.minimum((q_lo // CHUNK + 1) * CHUNK, lk)
    s_bufs, p_bufs = (s_a, s_b), (p_a, p_b)

    @pl.when(j == 0)
    def _():
        m_s[...] = jnp.full(m_s.shape, NEG_BIG, F32)
        l_s[...] = jnp.zeros(l_s.shape, F32)
        acc_s[...] = jnp.zeros(acc_s.shape, F32)

    def block(masked):
        if masked:
            qpos = q_lo + lax.broadcasted_iota(jnp.int32, (tq, tk), 0)
            kpos = k_lo + lax.broadcasted_iota(jnp.int32, (tq, tk), 1)
            visible = kpos < jnp.minimum((qpos // CHUNK + 1) * CHUNK, lk)
        def scores(h):
            q = q_ref[:, h * Q_SLAB:(h + 1) * Q_SLAB]
            s = _bdot(q, kt_ref[h * Q_SLAB:(h + 1) * Q_SLAB, :])
            if masked:
                s = jnp.where(visible, s, NEG_BIG)
            s_bufs[h % 2][...] = s
            m_prev = m_s[h]
            m_next = jnp.maximum(m_prev, jnp.max(s, axis=1, keepdims=True))
            m_s[h] = m_next
            return m_next, jnp.exp2(m_prev - m_next)

        def finish(h, m_next, alpha):
            s_buf, p_buf = s_bufs[h % 2], p_bufs[h % 2]
            for r in range(tq // strip):
                rows = slice(r * strip, (r + 1) * strip)
                pr = jnp.exp2(s_buf[rows, :] - m_next[rows, 0:1])
                part = pr[:, 0:LANE]
                for c in range(1, tk // LANE):
                    part = part + pr[:, c * LANE:(c + 1) * LANE]
                l_s[h, rows, :] = alpha[rows] * l_s[h, rows, :] + part
                p_buf[rows, :] = pr.astype(BF16)
            acc_s[h] = acc_s[h] * alpha + _bdot(p_buf[...], v_ref[:, h * V_DIM:(h + 1) * V_DIM])

        stats = scores(0)
        for h in range(N_HEADS_D):
            nxt = scores(h + 1) if h + 1 < N_HEADS_D else None
            finish(h, *stats)
            stats = nxt

    @pl.when(k_lo + tk <= all_visible)
    def _():
        block(False)

    @pl.when(k_lo + tk > all_visible)
    def _():
        block(True)

    @pl.when(fin_ref[step] == 1)
    def _():
        gate = _silu(gd_ref[...])
        for h in range(N_HEADS_D):
            sl = slice(h * V_DIM, (h + 1) * V_DIM)
            denom = jnp.sum(l_s[h], axis=1, keepdims=True)
            o_ref[:, sl] = (acc_s[h] / denom * gate[:, sl]).astype(o_ref.dtype)


def _attention(q, kt, v, p, q_off, lk):
    b, lq, _ = q.shape
    lkp = kt.shape[2]
    tq = _pick(lq, 512)
    tk = 1024 if lkp % 1024 == 0 else lkp
    strip = 16 if tq % 16 == 0 else SUBLANE
    nq = lq // tq
    qi, kj, fin = [], [], []
    for i in range(nq):
        last_bound = min(((q_off + (i + 1) * tq - 1) // CHUNK + 1) * CHUNK, lk)
        last_j = (last_bound - 1) // tk
        for j in range(last_j + 1):
            qi.append(i)
            kj.append(j)
            fin.append(int(j == last_j))
    to_arr = lambda a: jnp.asarray(np.asarray(a, np.int32))
    qw, vw = N_HEADS_D * Q_SLAB, N_HEADS_D * V_DIM
    grid_spec = pltpu.PrefetchScalarGridSpec(
        num_scalar_prefetch=3,
        grid=(b, len(qi)),
        in_specs=[pl.BlockSpec((None, tq, qw), lambda bi, s, qi_r, kj_r, fin_r: (bi, qi_r[s], 0)),
                  pl.BlockSpec((None, qw, tk), lambda bi, s, qi_r, kj_r, fin_r: (bi, 0, kj_r[s])),
                  pl.BlockSpec((None, tk, vw), lambda bi, s, qi_r, kj_r, fin_r: (bi, kj_r[s], 0)),
                  pl.BlockSpec((None, tq, vw), lambda bi, s, qi_r, kj_r, fin_r: (bi, qi_r[s], C_GD // vw))],
        out_specs=pl.BlockSpec((None, tq, vw), lambda bi, s, qi_r, kj_r, fin_r: (bi, qi_r[s], 0)),
        scratch_shapes=[pltpu.VMEM((N_HEADS_D, tq, LANE), F32), pltpu.VMEM((N_HEADS_D, tq, LANE), F32),
                        pltpu.VMEM((N_HEADS_D, tq, V_DIM), F32), pltpu.VMEM((tq, tk), F32), pltpu.VMEM((tq, tk), F32),
                        pltpu.VMEM((tq, tk), BF16), pltpu.VMEM((tq, tk), BF16)],
    )
    return pl.pallas_call(
        functools.partial(_attn_kernel, tq=tq, tk=tk, strip=strip, q_off=q_off, lk=lk),
        grid_spec=grid_spec,
        out_shape=jax.ShapeDtypeStruct((b, lq, vw), BF16),
        compiler_params=_params("parallel", "arbitrary"),
        name="mla_attention",
    )(to_arr(qi), to_arr(kj), to_arr(fin), q, kt, v, p)


def _prep_weights(pre_norm_w, w_in, conv_a_w, conv_a_b, lru_w_r, lru_b_r, lru_w_i, lru_b_i, lru_lambda, conv_b_w,
                  conv_c_w, conv_c_b, ssd_dt_bias, ssd_a_log, ssd_d, ssd_norm_w, mla_q_norm_w, mla_w_q_up,
                  mla_kv_norm_w, mla_w_kv_up, w_branch_out, w_out, post_norm_w):
    d = w_in.shape[0]
    w_all, w_narrow = _pack_w_in(w_in)

    wq = mla_w_q_up.reshape(d, Q_LORA, N_HEADS_D, NOPE_DIM + ROPE_DIM)
    nope, x1, x2 = wq[..., :NOPE_DIM], wq[..., NOPE_DIM:NOPE_DIM + HALF], wq[..., NOPE_DIM + HALF:]
    zq = lambda n: jnp.zeros((d, Q_LORA, N_HEADS_D, n), wq.dtype)
    wq_main = jnp.concatenate([nope, x1, x2, zq(Q_SLAB - NOPE_DIM - ROPE_DIM)], -1).reshape(d, Q_LORA, -1).astype(BF16)
    wkv = mla_w_kv_up.reshape(d, KV_LORA, N_HEADS_D, NOPE_DIM + V_DIM)
    wk = wkv[..., :NOPE_DIM].reshape(d, KV_LORA, -1).astype(BF16)
    wv = wkv[..., NOPE_DIM:].reshape(d, KV_LORA, -1).astype(BF16)

    pad_heads = lambda a: jnp.pad(a, ((0, 0), (0, LANE - N_HEADS_C)))[:, None, :]
    expand = (jnp.arange(LANE)[:, None] == (jnp.arange(BRANCH_W) // HEAD_DIM_C)[None, :]).astype(BF16)
    vec = lambda a: a[:, None, :]
    return dict(
        pre_norm_w=vec(pre_norm_w), post_norm_w=vec(post_norm_w), w_all=w_all, w_narrow=w_narrow,
        w_branch=w_branch_out.astype(BF16), w_out=w_out.astype(BF16),
        conv_a_w=conv_a_w, conv_a_b=vec(conv_a_b), lru_w_r=lru_w_r.astype(BF16), lru_b_r=vec(lru_b_r),
        lru_w_i=lru_w_i.astype(BF16), lru_b_i=vec(lru_b_i), lru_lambda=vec(lru_lambda), conv_b_w=conv_b_w,
        conv_cx_w=conv_c_w[..., :BRANCH_W], conv_cx_b=vec(conv_c_b[..., :BRANCH_W]),
        conv_cb_w=conv_c_w[..., BRANCH_W:], conv_cb_b=vec(conv_c_b[..., BRANCH_W:]),
        dt_bias=pad_heads(ssd_dt_bias), a_log=pad_heads(ssd_a_log), d_exp=vec(jnp.repeat(ssd_d, HEAD_DIM_C, axis=-1)),
        ssd_norm_w=vec(ssd_norm_w), expand=expand,
        mla_q_norm_w=vec(mla_q_norm_w), mla_kv_norm_w=vec(mla_kv_norm_w), wq=wq_main, wk=wk, wv=wv,
    )


def _rope_tables(pos, reps):
    inv = ROPE_THETA ** (-jnp.arange(HALF, dtype=F32) / HALF)
    ang = pos.astype(F32)[:, None] * inv[None, :]
    cos, sin = jnp.cos(ang), jnp.sin(ang)
    n = pos.shape[0]
    z = lambda w: jnp.zeros((n, w), F32)
    tabs = dict(
        cos_k=jnp.concatenate([cos, cos, z(LANE - ROPE_DIM)], 1),
        sin_k=jnp.concatenate([-sin, sin, z(LANE - ROPE_DIM)], 1),
    )
    return {k: jnp.tile(v, (reps, 1)) for k, v in tabs.items()}


def _layer(x, h, b, seq, tabs, q_off, state, past, w, l, last):
    conv_a_prev, h0, conv_b_prev, conv_c_prev, ssd0 = state
    m = b * seq
    p2d = _matmul(h, w["w_all"], l, PACK_W, 1024, F32, "in_proj")
    p = p2d.reshape(b, seq, PACK_W)
    pn2d = _matmul(h, w["w_narrow"], l, 2 * LANE, 2 * LANE, F32, "in_proj_narrow")
    pn = pn2d.reshape(b, seq, 2 * LANE)

    y_a, h_last, conv_a_new = _lru(p, conv_a_prev, h0[:, None, :], w, l)
    y_b, conv_b_new = _sconv(p, conv_b_prev, w, l)
    y_c, conv_cx_new, conv_cb_new, ssd_t = _ssd(p, pn, conv_c_prev[..., :BRANCH_W], conv_c_prev[..., BRANCH_W:],
                                                 jnp.swapaxes(ssd0, -1, -2), w, l)
    q, ckv, kpe, kpe128 = _mla_prep(p2d, pn2d, tabs, w, l)
    ckv_all, kpe_all, lk = ckv.reshape(b, seq, KV_LORA), kpe128.reshape(b, seq, LANE), seq
    if past is not None:
        past_ckv, past_kpe = past
        lk = past_ckv.shape[1] + seq
        lkp = -(-lk // LANE) * LANE
        tail = lambda wd: jnp.zeros((b, lkp - lk, wd), F32)
        ckv_all = jnp.concatenate([past_ckv, ckv_all, tail(KV_LORA)], axis=1)
        past_kpe128 = jnp.pad(past_kpe, ((0, 0), (0, 0), (0, LANE - ROPE_DIM)))
        kpe_all = jnp.concatenate([past_kpe128, kpe_all, tail(LANE)], axis=1)
    kt, v = _kv_expand(ckv_all, kpe_all, w, l)
    y_d = _attention(q.reshape(b, seq, -1), kt, v, p, q_off, lk)

    merged = _merge(h, [y.reshape(m, BRANCH_W) for y in (y_a, y_b, y_c, y_d)], w["w_all"], w["w_branch"], l)
    out = _matmul(merged, w["w_out"], l, D_MODEL, 1024, F32, "out_proj")
    x_new, h_next = _post(x, out, w["post_norm_w"], w["pre_norm_w"], l, last)
    new_state = (ckv.reshape(b, seq, KV_LORA), kpe.reshape(b, seq, ROPE_DIM), conv_a_new, h_last[:, 0, :], conv_b_new,
                 jnp.concatenate([conv_cx_new, conv_cb_new], axis=-1), jnp.swapaxes(ssd_t, -1, -2))
    return x_new, h_next, new_state


def kernel(x_prompt, x_sample, cache_mla_latent, cache_mla_kpe, state_lru_conv, state_lru_h, state_sconv, state_ssd_conv, state_ssd, pre_norm_w, w_in, conv_a_w, conv_a_b, lru_w_r, lru_b_r, lru_w_i, lru_b_i, lru_lambda, conv_b_w, conv_c_w, conv_c_b, ssd_dt_bias, ssd_a_log, ssd_d, ssd_norm_w, mla_q_norm_w, mla_w_q_up, mla_kv_norm_w, mla_w_kv_up, w_branch_out, w_out, post_norm_w):
    bp, sp, d = x_prompt.shape
    bs, ss, _ = x_sample.shape
    past = cache_mla_latent.shape[2]
    depth = w_in.shape[0]
    w = _prep_weights(pre_norm_w, w_in, conv_a_w, conv_a_b, lru_w_r, lru_b_r, lru_w_i, lru_b_i, lru_lambda, conv_b_w,
                      conv_c_w, conv_c_b, ssd_dt_bias, ssd_a_log, ssd_d, ssd_norm_w, mla_q_norm_w, mla_w_q_up,
                      mla_kv_norm_w, mla_w_kv_up, w_branch_out, w_out, post_norm_w)
    tabs_p = _rope_tables(jnp.arange(sp), bp)
    tabs_s = _rope_tables(past + jnp.arange(ss), bs)
    zeros_p = (jnp.zeros((bp, 3, BRANCH_W), F32), jnp.zeros((bp, BRANCH_W), F32), jnp.zeros((bp, 2, BRANCH_W), F32),
               jnp.zeros((bp, 3, BRANCH_W + BC_W), F32), jnp.zeros((bp, N_HEADS_C, HEAD_DIM_C, D_STATE), F32))

    xp = x_prompt.reshape(bp * sp, d)
    xs = x_sample.reshape(bs * ss, d)
    hp = _rms_cast(xp, w["pre_norm_w"], 0)
    hs = _rms_cast(xs, w["pre_norm_w"], 0)
    p_st = [[] for _ in range(7)]
    s_st = [[] for _ in range(7)]
    for l in range(depth):
        last = l == depth - 1
        xp, hp, new_p = _layer(xp, hp, bp, sp, tabs_p, 0, zeros_p, None, w, l, last)
        st_s = (state_lru_conv[l], state_lru_h[l], state_sconv[l], state_ssd_conv[l], state_ssd[l])
        xs, hs, new_s = _layer(xs, hs, bs, ss, tabs_s, past, st_s, (cache_mla_latent[l], cache_mla_kpe[l]), w, l, last)
        for i in range(7):
            p_st[i].append(new_p[i])
            s_st[i].append(new_s[i])
    p_out = [jnp.stack(v, axis=0) for v in p_st]
    s_out = [jnp.stack(v, axis=0) for v in s_st]
    return (xp.reshape(bp, sp, d), xs.reshape(bs, ss, d), *p_out, *s_out)
```
